```python
import math
import jax, jax.numpy as jnp
from jax import lax
import numpy as np

D_MODEL = 2048
BATCH = 2
SEQ = 4096
DEPTH = 1
DEC_BATCH = 128
DEC_SEQ = 4
PAST_LEN = 8192
PAGE_SIZE = 128

H_A = 8
Q_LORA = 512
KV_LORA = 512
NOPE_DIM = 128
ROPE_DIM = 64
V_DIM = 128
ROPE_THETA = 10000.0
SOFTMAX_SCALE = 1.0 / math.sqrt(NOPE_DIM + ROPE_DIM)
Q_BLOCK = 128
H_B = 8
DK_B = 128
DV_B = 128
CHUNK = 64
D_FF = 5632
PLE_DIM = 256
EPS = 1e-6
POOL_NUM = 5
POOL_DEN = 4

IN_SPLITS = (Q_LORA, KV_LORA, ROPE_DIM, H_B * DK_B, H_B * DK_B, H_B * DV_B, H_B * DV_B, D_MODEL, D_MODEL)
IN_COLS = Q_LORA + KV_LORA + ROPE_DIM + 2 * H_B * DK_B + 2 * H_B * DV_B + 2 * D_MODEL

kernel_name = 'mla_hgrn2_gated_macaron_decoder'


def split_points():
    return tuple(int(v) for v in np.cumsum(np.array(IN_SPLITS))[:-1])


def rmsnorm(x, g):
    xf = x.astype(jnp.float32)
    y = xf * lax.rsqrt(jnp.mean(xf * xf, axis=-1, keepdims=True) + EPS)
    return (y * g.astype(jnp.float32)).astype(x.dtype)


def swiglu(h, wg, wu, wd):
    return (jax.nn.silu(h @ wg) * (h @ wu)) @ wd


def rope(x, pos):
    half = ROPE_DIM // 2
    inv = ROPE_THETA ** (-jnp.arange(half, dtype=jnp.float32) / half)
    ang = pos[:, None] * inv[None, :]
    cos = jnp.cos(ang)[None, :, None, :]
    sin = jnp.sin(ang)[None, :, None, :]
    xf = x.astype(jnp.float32)
    x1, x2 = xf[..., :half], xf[..., half:]
    return jnp.concatenate([x1 * cos - x2 * sin, x1 * sin + x2 * cos], axis=-1).astype(x.dtype)


def mla_scores(q_lat, q_pe, ckv, kpe):
    s = jnp.einsum('bthc,bsc->bhts', q_lat, ckv) + jnp.einsum('bthr,bsr->bhts', q_pe, kpe)
    return s.astype(jnp.float32) * SOFTMAX_SCALE


def mla_prompt(q_lat, q_pe, ckv, kpe):
    B, T, H, C = q_lat.shape
    qb = min(Q_BLOCK, T)
    nb = T // qb
    qlb = q_lat.reshape(B, nb, qb, H, C).transpose(1, 0, 2, 3, 4)
    qpb = q_pe.reshape(B, nb, qb, H, ROPE_DIM).transpose(1, 0, 2, 3, 4)
    starts = jnp.arange(nb, dtype=jnp.int32) * qb
    k_pos = jnp.arange(T, dtype=jnp.int32)

    def block(args):
        ql, qp, st = args
        s = mla_scores(ql, qp, ckv, kpe)
        q_pos = st + jnp.arange(qb, dtype=jnp.int32)
        s = jnp.where(q_pos[:, None] >= k_pos[None, :], s, -jnp.inf)
        p = jax.nn.softmax(s, axis=-1).astype(ckv.dtype)
        return jnp.einsum('bhts,bsc->bthc', p, ckv)

    out = lax.map(block, (qlb, qpb, starts))
    return out.transpose(1, 0, 2, 3, 4).reshape(B, T, H, C)


def mla_sample(q_lat, q_pe, ckv_past, kpe_past, ckv_new, kpe_new):
    T = q_lat.shape[1]
    P = ckv_past.shape[1]
    s_past = mla_scores(q_lat, q_pe, ckv_past, kpe_past)
    s_new = mla_scores(q_lat, q_pe, ckv_new, kpe_new)
    causal = jnp.tril(jnp.ones((T, T), dtype=bool))
    s_new = jnp.where(causal, s_new, -jnp.inf)
    p = jax.nn.softmax(jnp.concatenate([s_past, s_new], axis=-1), axis=-1).astype(ckv_new.dtype)
    return (jnp.einsum('bhts,bsc->bthc', p[..., :P], ckv_past)
            + jnp.einsum('bhts,bsc->bthc', p[..., P:], ckv_new))


def hgrn2_recurrence(q, k, v, log_f, s0):
    B, T, H, _ = q.shape
    c = min(CHUNK, T)
    n = -(-T // c)
    pad = n * c - T

    def prep(a):
        a = jnp.pad(a.astype(jnp.float32), ((0, 0), (0, pad), (0, 0), (0, 0)))
        return a.reshape(B, n, c, H, a.shape[-1]).transpose(1, 0, 3, 2, 4)

    qc, kc, vc, gc = prep(q), prep(k), prep(v), prep(log_f)
    mask = jnp.tril(jnp.ones((c, c), dtype=bool))[:, :, None]

    def step(S, inp):
        qi, ki, vi, gi = inp
        b = jnp.cumsum(gi, axis=2)
        diff = b[:, :, :, None, :] - b[:, :, None, :, :]
        decay = jnp.exp(jnp.where(mask, diff, -jnp.inf))
        scores = jnp.einsum('bhtd,bhsd,bhtsd->bhts', qi, ki, decay)
        o = (jnp.einsum('bhts,bhsv->bhtv', scores, vi)
             + jnp.einsum('bhtd,bhdv->bhtv', qi * jnp.exp(b), S))
        b_last = b[:, :, -1:, :]
        S = (jnp.exp(b_last[:, :, 0, :])[..., None] * S
             + jnp.einsum('bhsd,bhsv->bhdv', ki * jnp.exp(b_last - b), vi))
        return S, o

    S, o = lax.scan(step, s0.astype(jnp.float32), (qc, kc, vc, gc))
    o = o.transpose(1, 0, 3, 2, 4).reshape(B, n * c, H, DV_B)[:, :T]
    return o, S


def layer(x, p_i, pos, lb, s0, past, lw):
    B, T, _ = x.shape
    h = rmsnorm(x, lw['ffn1_norm'])
    x = x + 0.5 * swiglu(h, lw['ffn1_w_gate'], lw['ffn1_w_up'], lw['ffn1_w_down'])
    h = rmsnorm(x, lw['mix_norm'])
    cq, ckv, kr, hq, hf, hi, hg, ga, gb = jnp.split(h @ lw['w_in'], split_points(), axis=-1)
    q = (rmsnorm(cq, lw['q_norm']) @ lw['w_uq']).reshape(B, T, H_A, NOPE_DIM + ROPE_DIM)
    q_pe = rope(q[..., NOPE_DIM:], pos)
    q_lat = jnp.einsum('bthn,chn->bthc', q[..., :NOPE_DIM], lw['w_uk'])
    ckv_new = rmsnorm(ckv, lw['kv_norm'])
    kpe_new = rope(kr[:, :, None, :], pos)[:, :, 0, :]
    if past is None:
        lat = mla_prompt(q_lat, q_pe, ckv_new, kpe_new)
    else:
        lat = mla_sample(q_lat, q_pe, past[0], past[1], ckv_new, kpe_new)
    o_a = jnp.einsum('bthc,chv->bthv', lat, lw['w_uv']).reshape(B, T, H_A * V_DIM)
    zf = hf.astype(jnp.float32)
    log_f = jnp.log(lb + (1.0 - lb) * jax.nn.sigmoid(zf))
    k = (1.0 - lb) * jax.nn.sigmoid(-zf)
    o_b, s_new = hgrn2_recurrence(hq.reshape(B, T, H_B, DK_B), k.reshape(B, T, H_B, DK_B),
                                  hi.reshape(B, T, H_B, DV_B), log_f.reshape(B, T, H_B, DK_B), s0)
    o_b = rmsnorm(o_b, lw['hgrn_out_norm']) * jax.nn.silu(hg.reshape(B, T, H_B, DV_B).astype(jnp.float32))
    o_b = o_b.astype(x.dtype).reshape(B, T, H_B * DV_B)
    merged = jax.nn.sigmoid(ga) * (o_a @ lw['w_branch_a']) + jax.nn.sigmoid(gb) * (o_b @ lw['w_branch_b'])
    x = x + merged @ lw['w_out']
    h = rmsnorm(x, lw['ffn2_norm'])
    x = x + 0.5 * swiglu(h, lw['ffn2_w_gate'], lw['ffn2_w_up'], lw['ffn2_w_down'])
    gate = jax.nn.sigmoid(rmsnorm(x, lw['ple_norm']) @ lw['w_ple_gate'])
    x = x + gate * (p_i.astype(x.dtype) @ lw['w_ple_proj'])
    return x, ckv_new, kpe_new, s_new.astype(x.dtype)


def setup_inputs(seed: int = 0) -> dict:
    key = jax.random.key(seed)
    ks = iter(jax.random.split(key, 48))

    def nrm(shape, scale):
        return scale * jax.random.normal(next(ks), shape, jnp.float32)

    def gain(n):
        return 1.0 + nrm((DEPTH, n), 0.01)

    n_pages = PAST_LEN // PAGE_SIZE
    n_pool = (DEC_BATCH * n_pages * POOL_NUM) // POOL_DEN
    page_table = jax.random.permutation(next(ks), n_pool)[:DEC_BATCH * n_pages]
    page_table = page_table.reshape(DEC_BATCH, n_pages).astype(jnp.int32)
    return {
        'x_prompt': nrm((BATCH, SEQ, D_MODEL), 1.0),
        'x_sample': nrm((DEC_BATCH, DEC_SEQ, D_MODEL), 1.0),
        'cache_ckv': nrm((DEPTH, n_pool, PAGE_SIZE, KV_LORA), 1.0),
        'cache_kpe': nrm((DEPTH, n_pool, PAGE_SIZE, ROPE_DIM), 1.0),
        'state_hgrn': nrm((DEPTH, DEC_BATCH, H_B, DK_B, DV_B), 0.3),
        'page_table': page_table,
        'p_prompt': nrm((DEPTH, BATCH, SEQ, PLE_DIM), 1.0),
        'p_sample': nrm((DEPTH, DEC_BATCH, DEC_SEQ, PLE_DIM), 1.0),
        'ffn1_norm': gain(D_MODEL),
        'ffn1_w_gate': nrm((DEPTH, D_MODEL, D_FF), D_MODEL ** -0.5),
        'ffn1_w_up': nrm((DEPTH, D_MODEL, D_FF), D_MODEL ** -0.5),
        'ffn1_w_down': nrm((DEPTH, D_FF, D_MODEL), D_FF ** -0.5),
        'mix_norm': gain(D_MODEL),
        'w_in': nrm((DEPTH, D_MODEL, IN_COLS), D_MODEL ** -0.5),
        'q_norm': gain(Q_LORA),
        'w_uq': nrm((DEPTH, Q_LORA, H_A * (NOPE_DIM + ROPE_DIM)), Q_LORA ** -0.5),
        'kv_norm': gain(KV_LORA),
        'w_uk': nrm((DEPTH, KV_LORA, H_A, NOPE_DIM), KV_LORA ** -0.5),
        'w_uv': nrm((DEPTH, KV_LORA, H_A, V_DIM), KV_LORA ** -0.5),
        'hgrn_lb_logits': nrm((DEPTH + 1, H_B * DK_B), 0.5),
        'hgrn_out_norm': gain(DV_B),
        'w_branch_a': nrm((DEPTH, H_A * V_DIM, D_MODEL), (H_A * V_DIM) ** -0.5),
        'w_branch_b': nrm((DEPTH, H_B * DV_B, D_MODEL), (H_B * DV_B) ** -0.5),
        'w_out': nrm((DEPTH, D_MODEL, D_MODEL), D_MODEL ** -0.5),
        'ffn2_norm': gain(D_MODEL),
        'ffn2_w_gate': nrm((DEPTH, D_MODEL, D_FF), D_MODEL ** -0.5),
        'ffn2_w_up': nrm((DEPTH, D_MODEL, D_FF), D_MODEL ** -0.5),
        'ffn2_w_down': nrm((DEPTH, D_FF, D_MODEL), D_FF ** -0.5),
        'ple_norm': gain(D_MODEL),
        'w_ple_gate': nrm((DEPTH, D_MODEL, D_MODEL), D_MODEL ** -0.5),
        'w_ple_proj': nrm((DEPTH, PLE_DIM, D_MODEL), PLE_DIM ** -0.5),
        'final_norm': 1.0 + nrm((D_MODEL,), 0.01),
    }


def reference(x_prompt, x_sample, cache_ckv, cache_kpe, state_hgrn, page_table, p_prompt, p_sample,
              ffn1_norm, ffn1_w_gate, ffn1_w_up, ffn1_w_down, mix_norm, w_in, q_norm, w_uq, kv_norm,
              w_uk, w_uv, hgrn_lb_logits, hgrn_out_norm, w_branch_a, w_branch_b, w_out,
              ffn2_norm, ffn2_w_gate, ffn2_w_up, ffn2_w_down, ple_norm, w_ple_gate, w_ple_proj, final_norm):
    n_dec, n_pages = page_table.shape
    past_len = n_pages * PAGE_SIZE
    b_prompt = x_prompt.shape[0]
    pos_prompt = jnp.arange(x_prompt.shape[1], dtype=jnp.float32)
    pos_sample = jnp.arange(x_sample.shape[1], dtype=jnp.float32) + past_len
    lb_all = jnp.cumsum(jax.nn.softmax(hgrn_lb_logits.astype(jnp.float32), axis=0), axis=0)
    xp, xs = x_prompt, x_sample
    ckv_p_l, kpe_p_l, s_p_l, ckv_s_l, kpe_s_l, s_s_l = [], [], [], [], [], []
    for i in range(DEPTH):
        lw = {
            'ffn1_norm': ffn1_norm[i], 'ffn1_w_gate': ffn1_w_gate[i], 'ffn1_w_up': ffn1_w_up[i],
            'ffn1_w_down': ffn1_w_down[i], 'mix_norm': mix_norm[i], 'w_in': w_in[i],
            'q_norm': q_norm[i], 'w_uq': w_uq[i], 'kv_norm': kv_norm[i], 'w_uk': w_uk[i], 'w_uv': w_uv[i],
            'hgrn_out_norm': hgrn_out_norm[i], 'w_branch_a': w_branch_a[i], 'w_branch_b': w_branch_b[i],
            'w_out': w_out[i], 'ffn2_norm': ffn2_norm[i], 'ffn2_w_gate': ffn2_w_gate[i],
            'ffn2_w_up': ffn2_w_up[i], 'ffn2_w_down': ffn2_w_down[i], 'ple_norm': ple_norm[i],
            'w_ple_gate': w_ple_gate[i], 'w_ple_proj': w_ple_proj[i],
        }
        lb = lb_all[i]
        s0_prompt = jnp.zeros((b_prompt, H_B, DK_B, DV_B), xp.dtype)
        xp, ckv_p, kpe_p, s_p = layer(xp, p_prompt[i], pos_prompt, lb, s0_prompt, None, lw)
        ckv_past = cache_ckv[i][page_table].reshape(n_dec, past_len, KV_LORA)
        kpe_past = cache_kpe[i][page_table].reshape(n_dec, past_len, ROPE_DIM)
        xs, ckv_s, kpe_s, s_s = layer(xs, p_sample[i], pos_sample, lb, state_hgrn[i], (ckv_past, kpe_past), lw)
        ckv_p_l.append(ckv_p); kpe_p_l.append(kpe_p); s_p_l.append(s_p)
        ckv_s_l.append(ckv_s); kpe_s_l.append(kpe_s); s_s_l.append(s_s)
    y_prompt = rmsnorm(xp, final_norm)
    y_sample = rmsnorm(xs, final_norm)
    return (y_prompt, y_sample, jnp.stack(ckv_p_l), jnp.stack(kpe_p_l), jnp.stack(s_p_l),
            jnp.stack(ckv_s_l), jnp.stack(kpe_s_l), jnp.stack(s_s_l))
```

```python
import functools
import math

import jax
import jax.numpy as jnp
from jax import lax
from jax.experimental import pallas as pl
from jax.experimental.pallas import tpu as pltpu

F32 = jnp.float32
BF16 = jnp.bfloat16

EPS = 1e-6
ROPE_THETA = 10000.0
PAGE_SIZE = 128
LANES = 128
HGRN_SUB = 16
HGRN_CHUNK = 64
ATTN_TQ = 128
ATTN_TK = 512
SAMPLE_PAGES = 16
SAMPLE_BUFS = 3
VMEM_LIMIT = 60 * 1024 * 1024


def _sigmoid(x):
    return 1.0 / (1.0 + jnp.exp(-x))


def _rms(x, g):
    return x * lax.rsqrt(jnp.mean(x * x, axis=-1, keepdims=True) + EPS) * g


def _dot(a, b):
    return jnp.dot(a, b, preferred_element_type=F32)


def _dot_nt(a, b):
    return lax.dot_general(a, b, (((1,), (1,)), ((), ())), preferred_element_type=F32)


def _dot_tn(a, b):
    return lax.dot_general(a, b, (((0,), (0,)), ((), ())), preferred_element_type=F32)


def _div_pow2(x, n):
    shift = int(math.log2(n))
    assert 2 ** shift == n
    return x >> shift


def _pick_tile(n, candidates):
    for c in candidates:
        if n % c == 0:
            return c
    raise ValueError(f"no tile in {candidates} divides {n}")


def _resident(shape):
    nd = len(shape)
    return pl.BlockSpec(shape, lambda *_: (0,) * nd, pipeline_mode=pl.Buffered(1))


def _ffn_kernel(*refs, nf, post):
    if post:
        x_ref, g_ref, wg_ref, wu_ref, wd_ref, g2_ref, o_ref, h2_ref, h_scr, acc_scr = refs
    else:
        x_ref, g_ref, wg_ref, wu_ref, wd_ref, o_ref, h_scr, acc_scr = refs
    f = pl.program_id(1)

    @pl.when(f == 0)
    def _():
        h_scr[...] = _rms(x_ref[...], g_ref[...]).astype(BF16)

    h = h_scr[...]
    g = _dot(h, wg_ref[...])
    u = _dot(h, wu_ref[...])
    a = (g * _sigmoid(g) * u).astype(BF16)
    d = _dot(a, wd_ref[...])

    @pl.when(f == 0)
    def _():
        acc_scr[...] = d

    @pl.when(f > 0)
    def _():
        acc_scr[...] += d

    @pl.when(f == nf - 1)
    def _():
        y = x_ref[...] + 0.5 * acc_scr[...]
        o_ref[...] = y
        if post:
            h2_ref[...] = _rms(y, g2_ref[...]).astype(BF16)


def _ffn(x, g, wg, wu, wd, g2=None):
    n, d = x.shape
    dff = wg.shape[1]
    tm = _pick_tile(n, (512, 256, 128))
    tf = _pick_tile(dff, (512, 256, 128))
    nf = dff // tf
    post = g2 is not None
    in_specs = [
        pl.BlockSpec((tm, d), lambda i, f: (i, 0)),
        pl.BlockSpec((1, d), lambda i, f: (0, 0)),
        pl.BlockSpec((d, tf), lambda i, f: (0, f)),
        pl.BlockSpec((d, tf), lambda i, f: (0, f)),
        pl.BlockSpec((tf, d), lambda i, f: (f, 0)),
    ]
    args = [x, g, wg, wu, wd]
    out_shape = [jax.ShapeDtypeStruct((n, d), F32)]
    out_specs = [pl.BlockSpec((tm, d), lambda i, f: (i, 0))]
    if post:
        in_specs.append(pl.BlockSpec((1, d), lambda i, f: (0, 0)))
        args.append(g2)
        out_shape.append(jax.ShapeDtypeStruct((n, d), BF16))
        out_specs.append(pl.BlockSpec((tm, d), lambda i, f: (i, 0)))
    res = pl.pallas_call(
        functools.partial(_ffn_kernel, nf=nf, post=post),
        grid=(n // tm, nf),
        in_specs=in_specs,
        out_specs=out_specs,
        out_shape=out_shape,
        scratch_shapes=[pltpu.VMEM((tm, d), BF16), pltpu.VMEM((tm, d), F32)],
        compiler_params=pltpu.CompilerParams(
            dimension_semantics=("parallel", "arbitrary"), vmem_limit_bytes=VMEM_LIMIT),
        name="ffn_post" if post else "ffn",
    )(*args)
    return res if post else res[0]


def _gates_kernel(h_ref, w_ref, o_ref):
    o_ref[...] = _sigmoid(_dot(h_ref[...], w_ref[...])).astype(BF16)


def _gates(h, w):
    n, d = h.shape
    nc = w.shape[1]
    tm = _pick_tile(n, (512, 256, 128))
    tn = _pick_tile(nc, (1024, 512, 256, 128))
    return pl.pallas_call(
        _gates_kernel,
        grid=(n // tm, nc // tn),
        in_specs=[pl.BlockSpec((tm, d), lambda i, j: (i, 0)),
                  pl.BlockSpec((d, tn), lambda i, j: (0, j))],
        out_specs=pl.BlockSpec((tm, tn), lambda i, j: (i, j)),
        out_shape=jax.ShapeDtypeStruct((n, nc), BF16),
        compiler_params=pltpu.CompilerParams(
            dimension_semantics=("parallel", "arbitrary"), vmem_limit_bytes=VMEM_LIMIT),
        name="gates",
    )(h, w)


def _hgrn_proj_kernel(h_ref, w_ref, lbl_ref, hv_ref, lf_ref, *, layer):
    j = pl.program_id(1)
    y = _dot(h_ref[...], w_ref[...])

    @pl.when(j < 2)
    def _():
        hv_ref[...] = y.astype(BF16)

    @pl.when(j == 2)
    def _():
        hv_ref[...] = (y * _sigmoid(y)).astype(BF16)

    @pl.when(j == 3)
    def _():
        logits = lbl_ref[...]
        e = jnp.exp(logits - jnp.max(logits, axis=0, keepdims=True))
        lb = jnp.sum(e[:layer + 1], axis=0, keepdims=True) / jnp.sum(e, axis=0, keepdims=True)
        lf_ref[...] = jnp.log(lb + (1.0 - lb) * _sigmoid(y))
        hv_ref[...] = ((1.0 - lb) * _sigmoid(-y)).astype(BF16)


def _hgrn_proj(h, w, lb_logits, layer):
    n, d = h.shape
    c = w.shape[1] // 4
    tm = _pick_tile(n, (512, 256, 128))
    return pl.pallas_call(
        functools.partial(_hgrn_proj_kernel, layer=layer),
        grid=(n // tm, 4),
        in_specs=[pl.BlockSpec((tm, d), lambda i, j: (i, 0)),
                  pl.BlockSpec((d, c), lambda i, j: (0, j)),
                  pl.BlockSpec(lb_logits.shape, lambda i, j: (0, 0))],
        out_specs=[pl.BlockSpec((tm, c), lambda i, j: (i, j)),
                   pl.BlockSpec((tm, c), lambda i, j: (i, 0))],
        out_shape=[jax.ShapeDtypeStruct((n, 4 * c), BF16),
                   jax.ShapeDtypeStruct((n, c), F32)],
        compiler_params=pltpu.CompilerParams(
            dimension_semantics=("parallel", "arbitrary"), vmem_limit_bytes=VMEM_LIMIT),
        name="hgrn_proj",
    )(h, w, lb_logits)


def _mla_prep_kernel(h_ref, wm_ref, qn_ref, kvn_ref, wuq_ref, wuk_ref, cs_ref,
                     ql_ref, qp_ref, ckvf_ref, ckvb_ref, kpef_ref, kpeb_ref,
                     *, heads, q_lora, kv_lora, nope, rope, scale):
    tm = h_ref.shape[0]
    c = _dot(h_ref[...], wm_ref[...])
    cos = cs_ref[:, :LANES]
    sin = cs_ref[:, LANES:]
    ckv = _rms(c[:, q_lora:q_lora + kv_lora], kvn_ref[...])
    ckvf_ref[...] = ckv
    ckvb_ref[...] = ckv.astype(BF16)
    o = q_lora + kv_lora
    kpe = c[:, o:o + LANES] * cos + c[:, o + LANES:o + 2 * LANES] * sin
    kpef_ref[...] = kpe[:, :rope]
    kpeb_ref[...] = kpe.astype(BF16)
    qn = _rms(c[:, :q_lora], qn_ref[...]).astype(BF16)
    q3 = _dot(qn, wuq_ref[...]) * scale
    nb = tm // LANES
    for h in range(heads):
        qh = q3[:, h * nope:(h + 1) * nope].astype(BF16)
        lat = _dot(qh, wuk_ref[h]).astype(BF16)
        ql_ref[:, h, :, :] = lat.reshape(nb, LANES, kv_lora)
        r0 = heads * nope + h * LANES
        r1 = heads * nope + heads * LANES + h * LANES
        pe = (q3[:, r0:r0 + LANES] * cos + q3[:, r1:r1 + LANES] * sin).astype(BF16)
        qp_ref[:, h, :, :] = pe.reshape(nb, LANES, LANES)


def _mla_prep(h, w_mla, q_norm, kv_norm, w_uq3, w_ukt, cs, *, heads, q_lora, kv_lora, nope, rope, scale):
    n, d = h.shape
    tm = _pick_tile(n, (256, 128))
    nb = tm // LANES
    kern = functools.partial(_mla_prep_kernel, heads=heads, q_lora=q_lora, kv_lora=kv_lora,
                             nope=nope, rope=rope, scale=scale)
    return pl.pallas_call(
        kern,
        grid=(n // tm,),
        in_specs=[pl.BlockSpec((tm, d), lambda i: (i, 0)),
                  _resident(w_mla.shape), _resident(q_norm.shape), _resident(kv_norm.shape),
                  _resident(w_uq3.shape), _resident(w_ukt.shape),
                  pl.BlockSpec((tm, 2 * LANES), lambda i: (i, 0))],
        out_specs=[pl.BlockSpec((nb, heads, LANES, kv_lora), lambda i: (i, 0, 0, 0)),
                   pl.BlockSpec((nb, heads, LANES, LANES), lambda i: (i, 0, 0, 0)),
                   pl.BlockSpec((tm, kv_lora), lambda i: (i, 0)),
                   pl.BlockSpec((tm, kv_lora), lambda i: (i, 0)),
                   pl.BlockSpec((tm, rope), lambda i: (i, 0)),
                   pl.BlockSpec((tm, LANES), lambda i: (i, 0))],
        out_shape=[jax.ShapeDtypeStruct((n // LANES, heads, LANES, kv_lora), BF16),
                   jax.ShapeDtypeStruct((n // LANES, heads, LANES, LANES), BF16),
                   jax.ShapeDtypeStruct((n, kv_lora), F32),
                   jax.ShapeDtypeStruct((n, kv_lora), BF16),
                   jax.ShapeDtypeStruct((n, rope), F32),
                   jax.ShapeDtypeStruct((n, LANES), BF16)],
        compiler_params=pltpu.CompilerParams(
            dimension_semantics=("parallel",), vmem_limit_bytes=VMEM_LIMIT),
        name="mla_prep",
    )(h, w_mla, q_norm, kv_norm, w_uq3, w_ukt, cs)


def _attn_prompt_kernel(ql_ref, qp_ref, k_ref, kp_ref, wuv_ref, o_ref, m_scr, l_scr, acc_scr,
                        *, tq, tk, heads, vdim):
    i = pl.program_id(1)
    q = ql_ref[0]
    qp = qp_ref[0]
    rows = q.shape[0]
    m_scr[...] = jnp.full(m_scr.shape, -jnp.inf, F32)
    l_scr[...] = jnp.zeros(l_scr.shape, F32)
    acc_scr[...] = jnp.zeros(acc_scr.shape, F32)

    def step(j, masked):
        off = pl.multiple_of(j * tk, tk)
        k = k_ref[0, pl.ds(off, tk), :]
        kp = kp_ref[0, pl.ds(off, tk), :]
        s = _dot_nt(q, k) + _dot_nt(qp, kp)
        if masked:
            tok = i * tq + (lax.broadcasted_iota(jnp.int32, (rows, 1), 0) & (tq - 1))
            col = off + lax.broadcasted_iota(jnp.int32, (1, tk), 1)
            s = jnp.where(col <= tok, s, -jnp.inf)
        m_prev = m_scr[...]
        m_new = jnp.maximum(m_prev, jnp.max(s, axis=-1, keepdims=True))
        alpha = jnp.exp(m_prev - m_new)
        p = jnp.exp(s - m_new)
        l_scr[...] = alpha * l_scr[...] + jnp.sum(p, axis=-1, keepdims=True)
        acc_scr[...] = alpha * acc_scr[...] + _dot(p.astype(BF16), k)
        m_scr[...] = m_new

    nfull = (i * tq) // tk

    def body(j, carry):
        step(j, False)
        return carry

    lax.fori_loop(0, nfull, body, 0)
    step(nfull, True)

    inv = 1.0 / l_scr[...]
    for h in range(heads):
        oh = (acc_scr[h * tq:(h + 1) * tq, :] * inv[h * tq:(h + 1) * tq]).astype(BF16)
        o_ref[:, h * vdim:(h + 1) * vdim] = _dot(oh, wuv_ref[:, h * vdim:(h + 1) * vdim]).astype(BF16)


def _attn_prompt(ql, qp, ckv, kpe, wuv, *, heads, vdim):
    b, t, c = ckv.shape
    tq = ATTN_TQ
    tk = min(ATTN_TK, t)
    nq = t // tq
    rows = heads * tq
    kern = functools.partial(_attn_prompt_kernel, tq=tq, tk=tk, heads=heads, vdim=vdim)
    return pl.pallas_call(
        kern,
        grid=(b, nq),
        in_specs=[pl.BlockSpec((1, rows, c), lambda bi, i: (bi * nq + i, 0, 0)),
                  pl.BlockSpec((1, rows, LANES), lambda bi, i: (bi * nq + i, 0, 0)),
                  pl.BlockSpec((1, t, c), lambda bi, i: (bi, 0, 0)),
                  pl.BlockSpec((1, t, LANES), lambda bi, i: (bi, 0, 0)),
                  _resident(wuv.shape)],
        out_specs=pl.BlockSpec((tq, heads * vdim), lambda bi, i: (bi * nq + i, 0)),
        out_shape=jax.ShapeDtypeStruct((b * t, heads * vdim), BF16),
        scratch_shapes=[pltpu.VMEM((rows, 1), F32), pltpu.VMEM((rows, 1), F32),
                        pltpu.VMEM((rows, c), F32)],
        compiler_params=pltpu.CompilerParams(
            dimension_semantics=("parallel", "arbitrary"), vmem_limit_bytes=VMEM_LIMIT),
        name="attn_prompt",
    )(ql, qp, ckv, kpe, wuv)


def _attn_sample_kernel(pt_ref, ql_ref, qp_ref, kn_ref, kpn_ref, wuv_ref, cck_ref, cpe_ref, o_ref,
                        bufk, bufp, sem, m_scr, l_scr, acc_scr,
                        *, nseq, npages, ch, nbuf, heads, vdim, ntok, rope):
    cps = npages // ch
    total = nseq * cps

    def page_copies(chunk, slot):
        seq = chunk // cps
        part = chunk % cps
        copies = []
        for pg in range(ch):
            page = pt_ref[seq * npages + part * ch + pg]
            copies.append(pltpu.make_async_copy(
                cck_ref.at[page], bufk.at[slot, pl.ds(pg * PAGE_SIZE, PAGE_SIZE), :], sem.at[0, slot]))
            copies.append(pltpu.make_async_copy(
                cpe_ref.at[page], bufp.at[slot, pl.ds(pg * PAGE_SIZE, PAGE_SIZE), :], sem.at[1, slot]))
        return copies

    def start_chunk(chunk):
        for cp in page_copies(chunk, chunk % nbuf):
            cp.start()

    def wait_chunk(chunk):
        for cp in page_copies(chunk, chunk % nbuf):
            cp.wait()

    for c0 in range(min(nbuf - 1, total)):
        start_chunk(c0)

    def softmax_update(s, kv):
        m_prev = m_scr[...]
        m_new = jnp.maximum(m_prev, jnp.max(s, axis=-1, keepdims=True))
        alpha = jnp.exp(m_prev - m_new)
        p = jnp.exp(s - m_new)
        l_scr[...] = alpha * l_scr[...] + jnp.sum(p, axis=-1, keepdims=True)
        acc_scr[...] = alpha * acc_scr[...] + _dot(p.astype(BF16), kv)
        m_scr[...] = m_new

    def body(c, carry):
        seq = c // cps
        part = c % cps
        slot = c % nbuf

        @pl.when(c + nbuf - 1 < total)
        def _():
            start_chunk(c + nbuf - 1)

        wait_chunk(c)

        @pl.when(part == 0)
        def _():
            m_scr[...] = jnp.full(m_scr.shape, -jnp.inf, F32)
            l_scr[...] = jnp.zeros(l_scr.shape, F32)
            acc_scr[...] = jnp.zeros(acc_scr.shape, F32)

        q = ql_ref[seq]
        qp = qp_ref[seq]
        kb = bufk[slot].astype(BF16)
        kpb = bufp[slot].astype(BF16)
        s = _dot_nt(q, kb) + _dot_nt(qp[:, :rope], kpb)
        softmax_update(s, kb)

        @pl.when(part == cps - 1)
        def _():
            kn = kn_ref[seq]
            sn = _dot_nt(q, kn) + _dot_nt(qp, kpn_ref[seq])
            rows, cols = sn.shape
            t_row = _div_pow2(lax.broadcasted_iota(jnp.int32, (rows, 1), 0), heads)
            col = lax.broadcasted_iota(jnp.int32, (1, cols), 1)
            sn = jnp.where((col <= t_row) & (col < ntok), sn, -jnp.inf)
            softmax_update(sn, kn)
            o = (acc_scr[...] / l_scr[...]).astype(BF16)
            y = _dot(o, wuv_ref[...])
            h_row = lax.broadcasted_iota(jnp.int32, (rows, 1), 0) & (heads - 1)
            h_col = _div_pow2(lax.broadcasted_iota(jnp.int32, (1, heads * vdim), 1), vdim)
            z = jnp.where(h_row == h_col, y, 0.0)
            tid = lax.broadcasted_iota(jnp.int32, (ntok, 1), 0)
            out = jnp.zeros((ntok, heads * vdim), F32)
            for t in range(ntok):
                zt = jnp.sum(z[t * heads:(t + 1) * heads], axis=0, keepdims=True)
                out = jnp.where(tid == t, zt, out)
            o_ref[seq] = out

        return carry

    lax.fori_loop(0, total, body, 0)


def _attn_sample(page_table, ql, qp, kn, kpn, wuv, cache_ckv, cache_kpe, *, heads, vdim, ntok):
    nseq, rows, c = ql.shape
    npages = page_table.shape[1]
    rope = cache_kpe.shape[-1]
    ch = min(SAMPLE_PAGES, npages)
    assert npages % ch == 0
    nbuf = SAMPLE_BUFS
    kern = functools.partial(_attn_sample_kernel, nseq=nseq, npages=npages, ch=ch, nbuf=nbuf,
                             heads=heads, vdim=vdim, ntok=ntok, rope=rope)
    vmem = pl.BlockSpec(memory_space=pltpu.VMEM)
    return pl.pallas_call(
        kern,
        in_specs=[pl.BlockSpec(memory_space=pltpu.SMEM), vmem, vmem, vmem, vmem, vmem,
                  pl.BlockSpec(memory_space=pl.ANY), pl.BlockSpec(memory_space=pl.ANY)],
        out_specs=vmem,
        out_shape=jax.ShapeDtypeStruct((nseq, ntok, heads * vdim), F32),
        scratch_shapes=[pltpu.VMEM((nbuf, ch * PAGE_SIZE, c), F32),
                        pltpu.VMEM((nbuf, ch * PAGE_SIZE, rope), F32),
                        pltpu.SemaphoreType.DMA((2, nbuf)),
                        pltpu.VMEM((rows, 1), F32), pltpu.VMEM((rows, 1), F32),
                        pltpu.VMEM((rows, c), F32)],
        compiler_params=pltpu.CompilerParams(vmem_limit_bytes=VMEM_LIMIT),
        name="attn_sample",
    )(page_table.reshape(-1), ql, qp, kn, kpn, wuv, cache_ckv, cache_kpe)


def _hgrn_kernel(hv_ref, lf_ref, s0_ref, gn_ref, ob_ref, so_ref,
                 s_scr, q_scr, k_scr, b_scr, v_scr, od_scr,
                 *, L, sub, n_t, heads, dk, dv):
    c = pl.program_id(1)
    nc = pl.num_programs(1)
    hd = heads * dk
    nsub = L // sub
    levels = int(math.log2(nsub))
    assert 2 ** levels == nsub

    @pl.when(c == 0)
    def _():
        s_scr[...] = s0_ref[0]

    g = lf_ref[...]
    g1 = g.astype(BF16)
    r1 = g - g1.astype(F32)
    g2 = r1.astype(BF16)
    g3 = (r1 - g2.astype(F32)).astype(BF16)
    row = lax.broadcasted_iota(jnp.int32, (L, L), 0)
    col = lax.broadcasted_iota(jnp.int32, (L, L), 1)
    mats = [col <= row]
    for lv in range(max(levels, 1)):
        same = _div_pow2(row, sub * 2 ** lv) == _div_pow2(col, sub * 2 ** lv)
        mats.append(same & (col <= row))
        mats.append(same)
    tall = jnp.concatenate([jnp.where(m, 1.0, 0.0).astype(BF16) for m in mats], axis=0)
    cs = _dot(tall, g1) + _dot(tall, g2) + _dot(tall, g3)
    b = cs[0:L]
    brel = cs[L:2 * L]
    bstart = b - brel
    bend = bstart + cs[2 * L:3 * L]
    blast = b[L - 1:L, :]

    q = hv_ref[:, 0:hd].astype(F32)
    v = hv_ref[:, hd:2 * hd]
    sg = hv_ref[:, 2 * hd:3 * hd].astype(F32)
    k = hv_ref[:, 3 * hd:4 * hd].astype(F32)

    q_scr[...] = q
    k_scr[...] = k
    b_scr[...] = brel
    v_scr[...] = v.astype(F32)

    qt = q * jnp.exp(brel)
    kh = k * jnp.exp(bend - b)
    q_in = (qt * jnp.exp(bstart)).astype(BF16)
    k_st = (kh * jnp.exp(blast - bend)).astype(BF16)
    e_last = jnp.exp(blast)

    lhs = [qt.astype(BF16)]
    rhs = [kh.astype(BF16)]
    valid = []
    for lv in range(levels):
        grp = sub * (2 ** lv)
        if lv > 0:
            gs = b - cs[(1 + 2 * lv) * L:(2 + 2 * lv) * L]
            ge = gs + cs[(2 + 2 * lv) * L:(3 + 2 * lv) * L]
            lhs.append((qt * jnp.exp(bstart - gs)).astype(BF16))
            rhs.append((kh * jnp.exp(ge - bend)).astype(BF16))
        rg = _div_pow2(row, grp)
        valid.append(((rg & 1) == 1) & (_div_pow2(col, grp) == rg - 1))

    rid = lax.broadcasted_iota(jnp.int32, (sub, 1), 0)

    def diag_body(i, carry):
        r0 = pl.multiple_of(i * sub, sub)
        for h in range(heads):
            ls = slice(h * dk, (h + 1) * dk)
            qb = q_scr[pl.ds(r0, sub), ls]
            kb = k_scr[pl.ds(r0, sub), ls]
            bb = b_scr[pl.ds(r0, sub), ls]
            vb = v_scr[pl.ds(r0, sub), ls]
            acc = jnp.zeros((sub, dv), F32)
            for t in range(n_t):
                nr = 8 if t < 8 else sub
                e = jnp.exp(jnp.minimum(bb[t:t + 1, :] - bb[:nr], 0.0))
                w = e * kb[:nr] * qb[t:t + 1, :]
                a = jnp.sum(w, axis=-1, keepdims=True)
                a = jnp.where(rid[:nr] <= t, a, 0.0)
                ot = jnp.sum(a * vb[:nr], axis=0, keepdims=True)
                acc = jnp.where(rid == t, ot, acc)
            od_scr[pl.ds(r0, sub), ls] = acc
        return carry

    lax.fori_loop(0, nsub, diag_body, 0)

    gn = gn_ref[...]
    for h in range(heads):
        ls = slice(h * dk, (h + 1) * dk)
        s_prev = s_scr[h]
        vh = v[:, ls]
        o = od_scr[:, ls] + _dot(q_in[:, ls], s_prev.astype(BF16))
        if levels > 0:
            a_off = jnp.zeros((L, L), F32)
            for lv in range(levels):
                a_off = a_off + jnp.where(valid[lv], _dot_nt(lhs[lv][:, ls], rhs[lv][:, ls]), 0.0)
            o = o + _dot(a_off.astype(BF16), vh)
        decay = jnp.transpose(jnp.broadcast_to(e_last[:, ls], (dk, dk)))
        s_scr[h] = decay * s_prev + _dot_tn(k_st[:, ls], vh)
        on = o * lax.rsqrt(jnp.mean(o * o, axis=-1, keepdims=True) + EPS) * gn
        ob_ref[:, ls] = (on * sg[:, ls]).astype(BF16)

    @pl.when(c == nc - 1)
    def _():
        so_ref[0] = s_scr[...]


def _hgrn(hv, lf, s0, gn, *, L, n_t, heads, dk, dv):
    nseq = s0.shape[0]
    rows = hv.shape[0]
    nc = rows // (nseq * L)
    hd = heads * dk
    sub = min(HGRN_SUB, L)
    kern = functools.partial(_hgrn_kernel, L=L, sub=sub, n_t=n_t, heads=heads, dk=dk, dv=dv)
    return pl.pallas_call(
        kern,
        grid=(nseq, nc),
        in_specs=[pl.BlockSpec((L, 4 * hd), lambda s, c: (s * nc + c, 0)),
                  pl.BlockSpec((L, hd), lambda s, c: (s * nc + c, 0)),
                  pl.BlockSpec((1, heads, dk, dv), lambda s, c: (s, 0, 0, 0)),
                  pl.BlockSpec((1, dv), lambda s, c: (0, 0))],
        out_specs=[pl.BlockSpec((L, heads * dv), lambda s, c: (s * nc + c, 0)),
                   pl.BlockSpec((1, heads, dk, dv), lambda s, c: (s, 0, 0, 0))],
        out_shape=[jax.ShapeDtypeStruct((rows, heads * dv), BF16),
                   jax.ShapeDtypeStruct((nseq, heads, dk, dv), F32)],
        scratch_shapes=[pltpu.VMEM((heads, dk, dv), F32)] + [pltpu.VMEM((L, hd), F32)] * 5,
        compiler_params=pltpu.CompilerParams(
            dimension_semantics=("parallel", "arbitrary"), vmem_limit_bytes=VMEM_LIMIT),
        name="hgrn",
    )(hv, lf, s0, gn)


def _merge_kernel(oa_ref, ob_ref, sg_ref, x_ref, wa_ref, wb_ref, wo_ref, o_ref, *, d):
    a = _dot(oa_ref[...], wa_ref[...])
    b = _dot(ob_ref[...], wb_ref[...])
    merged = sg_ref[:, :d].astype(F32) * a + sg_ref[:, d:].astype(F32) * b
    o_ref[...] = x_ref[...] + _dot(merged.astype(BF16), wo_ref[...])


def _merge(oa, ob, sg, x, wa, wb, wo):
    n, d = x.shape
    tm = _pick_tile(n, (256, 128))
    return pl.pallas_call(
        functools.partial(_merge_kernel, d=d),
        grid=(n // tm,),
        in_specs=[pl.BlockSpec((tm, oa.shape[1]), lambda i: (i, 0)),
                  pl.BlockSpec((tm, ob.shape[1]), lambda i: (i, 0)),
                  pl.BlockSpec((tm, 2 * d), lambda i: (i, 0)),
                  pl.BlockSpec((tm, d), lambda i: (i, 0)),
                  _resident(wa.shape), _resident(wb.shape), _resident(wo.shape)],
        out_specs=pl.BlockSpec((tm, d), lambda i: (i, 0)),
        out_shape=jax.ShapeDtypeStruct((n, d), F32),
        compiler_params=pltpu.CompilerParams(
            dimension_semantics=("parallel",), vmem_limit_bytes=VMEM_LIMIT),
        name="merge_out",
    )(oa, ob, sg, x, wa, wb, wo)


def _ple_kernel(x_ref, p_ref, g_ref, wg_ref, wp_ref, gf_ref, o_ref, *, final):
    x = x_ref[...]
    gate = _sigmoid(_dot(_rms(x, g_ref[...]).astype(BF16), wg_ref[...]))
    y = x + gate * _dot(p_ref[...].astype(BF16), wp_ref[...])
    o_ref[...] = _rms(y, gf_ref[...]) if final else y


def _ple(x, p, g, wg, wp, gf, final):
    n, d = x.shape
    tm = _pick_tile(n, (256, 128))
    return pl.pallas_call(
        functools.partial(_ple_kernel, final=final),
        grid=(n // tm,),
        in_specs=[pl.BlockSpec((tm, d), lambda i: (i, 0)),
                  pl.BlockSpec((tm, p.shape[1]), lambda i: (i, 0)),
                  _resident(g.shape), _resident(wg.shape), _resident(wp.shape), _resident(gf.shape)],
        out_specs=pl.BlockSpec((tm, d), lambda i: (i, 0)),
        out_shape=jax.ShapeDtypeStruct((n, d), F32),
        compiler_params=pltpu.CompilerParams(
            dimension_semantics=("parallel",), vmem_limit_bytes=VMEM_LIMIT),
        name="ple",
    )(x, p, g, wg, wp, gf)


def _rot_cols(w):
    half = w.shape[-1] // 2
    return jnp.concatenate([-w[..., half:], w[..., :half]], axis=-1)


def _pad_lanes(w):
    return jnp.pad(w, [(0, 0)] * (w.ndim - 1) + [(0, LANES - w.shape[-1])])


def kernel(x_prompt, x_sample, cache_ckv, cache_kpe, state_hgrn, page_table, p_prompt, p_sample, ffn1_norm, ffn1_w_gate, ffn1_w_up, ffn1_w_down, mix_norm, w_in, q_norm, w_uq, kv_norm, w_uk, w_uv, hgrn_lb_logits, hgrn_out_norm, w_branch_a, w_branch_b, w_out, ffn2_norm, ffn2_w_gate, ffn2_w_up, ffn2_w_down, ple_norm, w_ple_gate, w_ple_proj, final_norm):
    bsz, seq, d = x_prompt.shape
    nseq, ntok, _ = x_sample.shape
    depth = w_in.shape[0]
    q_lora = q_norm.shape[1]
    kv_lora, heads, nope = w_uk.shape[1:]
    vdim = w_uv.shape[-1]
    rope = cache_kpe.shape[-1]
    hb, dk, dv = state_hgrn.shape[2:]
    hd = hb * dk
    npages = page_table.shape[1]
    past_len = npages * PAGE_SIZE
    ptok = bsz * seq
    stok = nseq * ntok
    scale = 1.0 / math.sqrt(nope + rope)
    assert rope <= LANES and nope == LANES and dk == LANES and dv == LANES

    x = jnp.concatenate([x_prompt.reshape(ptok, d), x_sample.reshape(stok, d)], axis=0)

    half = rope // 2
    inv = ROPE_THETA ** (-jnp.arange(half, dtype=F32) / half)
    pos = jnp.concatenate([jnp.tile(jnp.arange(seq, dtype=F32), bsz),
                           jnp.tile(jnp.arange(ntok, dtype=F32) + past_len, nseq)])
    ang = pos[:, None] * inv[None, :]
    cos = jnp.cos(ang)
    sin = jnp.sin(ang)
    cs = jnp.concatenate([_pad_lanes(jnp.concatenate([cos, cos], axis=1)),
                          _pad_lanes(jnp.concatenate([sin, sin], axis=1))], axis=1)

    sp = [0]
    for w in (q_lora, kv_lora, rope, hd, hd, hb * dv, hb * dv, d, d):
        sp.append(sp[-1] + w)

    ckv_p, kpe_p, st_p, ckv_s, kpe_s, st_s = [], [], [], [], [], []
    for i in range(depth):
        wi = w_in[i]
        seg = [wi[:, sp[j]:sp[j + 1]] for j in range(9)]
        w_mla = jnp.concatenate([seg[0], seg[1], _pad_lanes(seg[2]), _pad_lanes(_rot_cols(seg[2]))],
                                axis=1).astype(BF16)
        w_hg = jnp.concatenate([seg[3], seg[5], seg[6], seg[4]], axis=1).astype(BF16)
        w_gt = jnp.concatenate([seg[7], seg[8]], axis=1).astype(BF16)
        wq = w_uq[i].reshape(q_lora, heads, nope + rope)
        wq_rope = wq[:, :, nope:]
        w_uq3 = jnp.concatenate([wq[:, :, :nope].reshape(q_lora, heads * nope),
                                 _pad_lanes(wq_rope).reshape(q_lora, heads * LANES),
                                 _pad_lanes(_rot_cols(wq_rope)).reshape(q_lora, heads * LANES)],
                                axis=1).astype(BF16)
        w_ukt = jnp.transpose(w_uk[i], (1, 2, 0)).astype(BF16)
        w_uv2 = w_uv[i].reshape(kv_lora, heads * vdim).astype(BF16)

        x, hmix = _ffn(x, ffn1_norm[i][None], ffn1_w_gate[i].astype(BF16), ffn1_w_up[i].astype(BF16),
                       ffn1_w_down[i].astype(BF16), mix_norm[i][None])

        sg = _gates(hmix, w_gt)
        hv, lf = _hgrn_proj(hmix, w_hg, hgrn_lb_logits, i)
        ql, qp, ckv_f, ckv_b, kpe_f, kpe_b = _mla_prep(
            hmix, w_mla, q_norm[i][None], kv_norm[i][None], w_uq3, w_ukt, cs,
            heads=heads, q_lora=q_lora, kv_lora=kv_lora, nope=nope, rope=rope, scale=scale)

        npb = ptok // LANES
        oa_p = _attn_prompt(ql[:npb].reshape(npb, heads * LANES, kv_lora),
                            qp[:npb].reshape(npb, heads * LANES, LANES),
                            ckv_b[:ptok].reshape(bsz, seq, kv_lora),
                            kpe_b[:ptok].reshape(bsz, seq, LANES), w_uv2, heads=heads, vdim=vdim)

        def sample_rows(a):
            c_ = a.shape[-1]
            return jnp.transpose(a[npb:], (0, 2, 1, 3)).reshape(nseq, ntok * heads, c_)

        def new_keys(a):
            c_ = a.shape[-1]
            return jnp.pad(a.reshape(nseq, ntok, c_), ((0, 0), (0, 16 - ntok), (0, 0)))

        oa_s = _attn_sample(page_table, sample_rows(ql), sample_rows(qp),
                            new_keys(ckv_b[ptok:]), new_keys(kpe_b[ptok:]), w_uv2,
                            cache_ckv[i], cache_kpe[i], heads=heads, vdim=vdim, ntok=ntok)
        oa = jnp.concatenate([oa_p, oa_s.reshape(stok, heads * vdim).astype(BF16)], axis=0)

        gn = hgrn_out_norm[i][None]
        ob_p, s_p = _hgrn(hv[:ptok], lf[:ptok], jnp.zeros((bsz, hb, dk, dv), F32), gn,
                          L=HGRN_CHUNK, n_t=HGRN_SUB, heads=hb, dk=dk, dv=dv)
        pad = HGRN_SUB - ntok
        hv_s = jnp.pad(hv[ptok:].reshape(nseq, ntok, 4 * hd), ((0, 0), (0, pad), (0, 0)))
        lf_s = jnp.pad(lf[ptok:].reshape(nseq, ntok, hd), ((0, 0), (0, pad), (0, 0)))
        ob_s, s_s = _hgrn(hv_s.reshape(nseq * HGRN_SUB, 4 * hd), lf_s.reshape(nseq * HGRN_SUB, hd),
                          state_hgrn[i], gn, L=HGRN_SUB, n_t=ntok, heads=hb, dk=dk, dv=dv)
        ob = jnp.concatenate([ob_p, ob_s.reshape(nseq, HGRN_SUB, hb * dv)[:, :ntok].reshape(stok, hb * dv)],
                             axis=0)

        x = _merge(oa, ob, sg, x, w_branch_a[i].astype(BF16), w_branch_b[i].astype(BF16),
                   w_out[i].astype(BF16))
        x = _ffn(x, ffn2_norm[i][None], ffn2_w_gate[i].astype(BF16), ffn2_w_up[i].astype(BF16),
                 ffn2_w_down[i].astype(BF16))
        p_i = jnp.concatenate([p_prompt[i].reshape(ptok, -1), p_sample[i].reshape(stok, -1)], axis=0)
        last = i == depth - 1
        x = _ple(x, p_i, ple_norm[i][None], w_ple_gate[i].astype(BF16), w_ple_proj[i].astype(BF16),
                 final_norm[None], last)

        ckv_p.append(ckv_f[:ptok].reshape(bsz, seq, kv_lora))
        kpe_p.append(kpe_f[:ptok].reshape(bsz, seq, rope))
        st_p.append(s_p)
        ckv_s.append(ckv_f[ptok:].reshape(nseq, ntok, kv_lora))
        kpe_s.append(kpe_f[ptok:].reshape(nseq, ntok, rope))
        st_s.append(s_s)

    return (x[:ptok].reshape(bsz, seq, d), x[ptok:].reshape(nseq, ntok, d),
            jnp.stack(ckv_p), jnp.stack(kpe_p), jnp.stack(st_p),
            jnp.stack(ckv_s), jnp.stack(kpe_s), jnp.stack(st_s))
```

```python
import functools
import math

import jax
import jax.numpy as jnp
from jax import lax
from jax.experimental import pallas as pl
from jax.experimental.pallas import tpu as pltpu

F32 = jnp.float32
BF16 = jnp.bfloat16

EPS = 1e-6
ROPE_THETA = 10000.0
PAGE_SIZE = 128
LANES = 128
HGRN_SUB = 16
HGRN_CHUNK = 64
HGRN_SAMPLE_ROWS = 16
ATTN_TQ = 128
ATTN_TK = 512
ATTN_ROW_CHUNKS = 4
SAMPLE_PAGES = 8
SAMPLE_BUFS = 4
VMEM_LIMIT = 60 * 1024 * 1024


def _sigmoid(x):
    return 1.0 / (1.0 + jnp.exp(-x))


def _rms(x, g):
    return x * lax.rsqrt(jnp.mean(x * x, axis=-1, keepdims=True) + EPS) * g


def _dot(a, b):
    return jnp.dot(a, b, preferred_element_type=F32)


def _dot_nt(a, b):
    return lax.dot_general(a, b, (((1,), (1,)), ((), ())), preferred_element_type=F32)


def _dot_tn(a, b):
    return lax.dot_general(a, b, (((0,), (0,)), ((), ())), preferred_element_type=F32)


def _div_pow2(x, n):
    shift = int(math.log2(n))
    assert 2 ** shift == n
    return x >> shift


def _pick_tile(sizes, candidates):
    for c in candidates:
        if all(n % c == 0 for n in sizes):
            return c
    raise ValueError(f"no tile in {candidates} divides {sizes}")


def _resident(shape):
    nd = len(shape)
    return pl.BlockSpec(shape, lambda *_: (0,) * nd, pipeline_mode=pl.Buffered(1))


def _split_specs(tm, width, npb, ngrid):
    if ngrid == 1:
        return [pl.BlockSpec((tm, width), lambda i: (jnp.minimum(i, npb - 1), 0)),
                pl.BlockSpec((tm, width), lambda i: (jnp.maximum(i - npb, 0), 0))]
    return [pl.BlockSpec((tm, width), lambda i, j: (jnp.minimum(i, npb - 1), 0)),
            pl.BlockSpec((tm, width), lambda i, j: (jnp.maximum(i - npb, 0), 0))]


def _ffn_kernel(*refs, nf, post, npb):
    refs = list(refs)
    split = npb is not None
    xp_ref = refs.pop(0)
    xs_ref = refs.pop(0) if split else None
    g_ref, wg_ref, wu_ref, wd_ref = refs[:4]
    refs = refs[4:]
    g2_ref = refs.pop(0) if post else None
    o_ref = refs.pop(0)
    h2_ref = refs.pop(0) if post else None
    h_scr, acc_scr = refs[:2]
    x_scr = refs[2] if split else xp_ref
    i = pl.program_id(0)
    f = pl.program_id(1)

    @pl.when(f == 0)
    def _():
        if split:
            @pl.when(i < npb)
            def _():
                x_scr[...] = xp_ref[...]

            @pl.when(i >= npb)
            def _():
                x_scr[...] = xs_ref[...]
        h_scr[...] = _rms(x_scr[...], g_ref[...]).astype(BF16)

    h = h_scr[...]
    g = _dot(h, wg_ref[...])
    u = _dot(h, wu_ref[...])
    a = (g * _sigmoid(g) * u).astype(BF16)
    d = _dot(a, wd_ref[...])

    @pl.when(f == 0)
    def _():
        acc_scr[...] = d

    @pl.when(f > 0)
    def _():
        acc_scr[...] += d

    @pl.when(f == nf - 1)
    def _():
        y = x_scr[...] + 0.5 * acc_scr[...]
        o_ref[...] = y
        if post:
            h2_ref[...] = _rms(y, g2_ref[...]).astype(BF16)


def _ffn(x, g, wg, wu, wd, g2=None):
    split = isinstance(x, tuple)
    parts = x if split else (x,)
    d = parts[0].shape[1]
    n = sum(p.shape[0] for p in parts)
    dff = wg.shape[1]
    tm = _pick_tile([p.shape[0] for p in parts], (512, 256, 128))
    tf = _pick_tile([dff], (512, 256, 128))
    nf = dff // tf
    npb = parts[0].shape[0] // tm if split else None
    post = g2 is not None
    if split:
        in_specs = _split_specs(tm, d, npb, 2)
    else:
        in_specs = [pl.BlockSpec((tm, d), lambda i, f: (i, 0))]
    in_specs += [
        pl.BlockSpec((1, d), lambda i, f: (0, 0)),
        pl.BlockSpec((d, tf), lambda i, f: (0, f)),
        pl.BlockSpec((d, tf), lambda i, f: (0, f)),
        pl.BlockSpec((tf, d), lambda i, f: (f, 0)),
    ]
    args = list(parts) + [g, wg, wu, wd]
    out_shape = [jax.ShapeDtypeStruct((n, d), F32)]
    out_specs = [pl.BlockSpec((tm, d), lambda i, f: (i, 0))]
    if post:
        in_specs.append(pl.BlockSpec((1, d), lambda i, f: (0, 0)))
        args.append(g2)
        out_shape.append(jax.ShapeDtypeStruct((n, d), BF16))
        out_specs.append(pl.BlockSpec((tm, d), lambda i, f: (i, 0)))
    scratch = [pltpu.VMEM((tm, d), BF16), pltpu.VMEM((tm, d), F32)]
    if split:
        scratch.append(pltpu.VMEM((tm, d), F32))
    res = pl.pallas_call(
        functools.partial(_ffn_kernel, nf=nf, post=post, npb=npb),
        grid=(n // tm, nf),
        in_specs=in_specs,
        out_specs=out_specs,
        out_shape=out_shape,
        scratch_shapes=scratch,
        compiler_params=pltpu.CompilerParams(
            dimension_semantics=("parallel", "arbitrary"), vmem_limit_bytes=VMEM_LIMIT),
        name="ffn_post" if post else "ffn",
    )(*args)
    return res if post else res[0]


def _gates_kernel(h_ref, w_ref, o_ref):
    o_ref[...] = _sigmoid(_dot_nt(h_ref[...], w_ref[...])).astype(BF16)


def _gates(h, wt):
    n, d = h.shape
    nc = wt.shape[0]
    tm = _pick_tile([n], (512, 256, 128))
    tn = _pick_tile([nc], (1024, 512, 256, 128))
    return pl.pallas_call(
        _gates_kernel,
        grid=(n // tm, nc // tn),
        in_specs=[pl.BlockSpec((tm, d), lambda i, j: (i, 0)),
                  pl.BlockSpec((tn, d), lambda i, j: (j, 0))],
        out_specs=pl.BlockSpec((tm, tn), lambda i, j: (i, j)),
        out_shape=jax.ShapeDtypeStruct((n, nc), BF16),
        compiler_params=pltpu.CompilerParams(
            dimension_semantics=("parallel", "arbitrary"), vmem_limit_bytes=VMEM_LIMIT),
        name="gates",
    )(h, wt)


def _hgrn_proj_kernel(h_ref, w_ref, lbl_ref, hv_ref, lf_ref, *, layer):
    j = pl.program_id(1)
    y = _dot_nt(h_ref[...], w_ref[...])

    @pl.when(j < 2)
    def _():
        hv_ref[...] = y.astype(BF16)

    @pl.when(j == 2)
    def _():
        hv_ref[...] = (y * _sigmoid(y)).astype(BF16)

    @pl.when(j == 3)
    def _():
        logits = lbl_ref[...]
        e = jnp.exp(logits - jnp.max(logits, axis=0, keepdims=True))
        lb = jnp.sum(e[:layer + 1], axis=0, keepdims=True) / jnp.sum(e, axis=0, keepdims=True)
        lf_ref[...] = jnp.log(lb + (1.0 - lb) * _sigmoid(y))
        hv_ref[...] = ((1.0 - lb) * _sigmoid(-y)).astype(BF16)


def _hgrn_proj(h, wt, lb_logits, layer):
    n, d = h.shape
    c = wt.shape[0] // 4
    tm = _pick_tile([n], (512, 256, 128))
    return pl.pallas_call(
        functools.partial(_hgrn_proj_kernel, layer=layer),
        grid=(n // tm, 4),
        in_specs=[pl.BlockSpec((tm, d), lambda i, j: (i, 0)),
                  pl.BlockSpec((c, d), lambda i, j: (j, 0)),
                  pl.BlockSpec(lb_logits.shape, lambda i, j: (0, 0))],
        out_specs=[pl.BlockSpec((tm, c), lambda i, j: (i, j)),
                   pl.BlockSpec((tm, c), lambda i, j: (i, 0))],
        out_shape=[jax.ShapeDtypeStruct((n, 4 * c), BF16),
                   jax.ShapeDtypeStruct((n, c), F32)],
        compiler_params=pltpu.CompilerParams(
            dimension_semantics=("parallel", "arbitrary"), vmem_limit_bytes=VMEM_LIMIT),
        name="hgrn_proj",
    )(h, wt, lb_logits)


def _mla_prep_kernel(h_ref, wm_ref, qn_ref, kvn_ref, wuq_ref, wuk_ref, cs_ref,
                     qc_ref, kc_ref, ckvp_ref, ckvs_ref, kpep_ref, kpes_ref,
                     *, heads, q_lora, kv_lora, nope, rope, scale, npb):
    i = pl.program_id(0)
    tm = h_ref.shape[0]
    c = _dot_nt(h_ref[...], wm_ref[...])
    cos = cs_ref[:, :LANES]
    sin = cs_ref[:, LANES:]
    ckv = _rms(c[:, q_lora:q_lora + kv_lora], kvn_ref[...])
    o = q_lora + kv_lora
    kpe = c[:, o:o + LANES] * cos + c[:, o + LANES:o + 2 * LANES] * sin
    kc_ref[:, :kv_lora] = ckv.astype(BF16)
    kc_ref[:, kv_lora:] = kpe.astype(BF16)

    @pl.when(i < npb)
    def _():
        ckvp_ref[...] = ckv
        kpep_ref[...] = kpe[:, :rope]

    @pl.when(i >= npb)
    def _():
        ckvs_ref[...] = ckv
        kpes_ref[...] = kpe[:, :rope]

    qn = _rms(c[:, :q_lora], qn_ref[...]).astype(BF16)
    q3 = _dot(qn, wuq_ref[...]) * scale
    nb = tm // LANES
    for h in range(heads):
        qh = q3[:, h * nope:(h + 1) * nope].astype(BF16)
        lat = _dot(qh, wuk_ref[h]).astype(BF16)
        qc_ref[:, h, :, :kv_lora] = lat.reshape(nb, LANES, kv_lora)
        r0 = heads * nope + h * LANES
        r1 = heads * nope + heads * LANES + h * LANES
        pe = (q3[:, r0:r0 + LANES] * cos + q3[:, r1:r1 + LANES] * sin).astype(BF16)
        qc_ref[:, h, :, kv_lora:] = pe.reshape(nb, LANES, LANES)


def _mla_prep(h, w_mlat, q_norm, kv_norm, w_uq3, w_ukt, cs, ptok, *, heads, q_lora, kv_lora, nope, rope, scale):
    n, d = h.shape
    stok = n - ptok
    tm = _pick_tile([ptok, stok], (256, 128))
    nb = tm // LANES
    npb = ptok // tm
    cw = kv_lora + LANES
    kern = functools.partial(_mla_prep_kernel, heads=heads, q_lora=q_lora, kv_lora=kv_lora,
                             nope=nope, rope=rope, scale=scale, npb=npb)
    return pl.pallas_call(
        kern,
        grid=(n // tm,),
        in_specs=[pl.BlockSpec((tm, d), lambda i: (i, 0)),
                  _resident(w_mlat.shape), _resident(q_norm.shape), _resident(kv_norm.shape),
                  _resident(w_uq3.shape), _resident(w_ukt.shape),
                  pl.BlockSpec((tm, 2 * LANES), lambda i: (i, 0))],
        out_specs=[pl.BlockSpec((nb, heads, LANES, cw), lambda i: (i, 0, 0, 0)),
                   pl.BlockSpec((tm, cw), lambda i: (i, 0))]
                  + _split_specs(tm, kv_lora, npb, 1) + _split_specs(tm, rope, npb, 1),
        out_shape=[jax.ShapeDtypeStruct((n // LANES, heads, LANES, cw), BF16),
                   jax.ShapeDtypeStruct((n, cw), BF16),
                   jax.ShapeDtypeStruct((ptok, kv_lora), F32),
                   jax.ShapeDtypeStruct((stok, kv_lora), F32),
                   jax.ShapeDtypeStruct((ptok, rope), F32),
                   jax.ShapeDtypeStruct((stok, rope), F32)],
        compiler_params=pltpu.CompilerParams(
            dimension_semantics=("arbitrary",), vmem_limit_bytes=VMEM_LIMIT),
        name="mla_prep",
    )(h, w_mlat, q_norm, kv_norm, w_uq3, w_ukt, cs)


def _attn_prompt_kernel(q_ref, k_ref, wuv_ref, o_ref, m_scr, l_scr, acc_scr,
                        *, tq, tk, heads, vdim, c, nchunk):
    i = pl.program_id(1)
    hpc = heads // nchunk
    rc = hpc * tq
    m_scr[...] = jnp.full(m_scr.shape, -jnp.inf, F32)
    l_scr[...] = jnp.zeros(l_scr.shape, F32)
    acc_scr[...] = jnp.zeros(acc_scr.shape, F32)

    nt = tk // LANES

    def step(j, masked):
        off = pl.multiple_of(j * tk, tk)
        k = k_ref[pl.ds(off, tk), :]
        v = k[:, :c]
        if masked:
            tok = i * tq + (lax.broadcasted_iota(jnp.int32, (rc, 1), 0) & (tq - 1))
            col = off + lax.broadcasted_iota(jnp.int32, (1, tk), 1)
            keep = col <= tok

        def scores(r):
            q = q_ref[0, r * hpc:(r + 1) * hpc].reshape(rc, q_ref.shape[-1])
            s = _dot_nt(q, k)
            return jnp.where(keep, s, -jnp.inf) if masked else s

        def update(r, s):
            rs = slice(r * rc, (r + 1) * rc)
            m_prev = m_scr[rs]
            smax = s[:, :LANES]
            for t in range(1, nt):
                smax = jnp.maximum(smax, s[:, t * LANES:(t + 1) * LANES])
            m_new = jnp.maximum(m_prev, jnp.max(smax, axis=-1, keepdims=True))
            alpha = jnp.exp2(m_prev - m_new)
            p = jnp.exp2(s - jnp.concatenate([m_new] * nt, axis=1))
            psum = p[:, :LANES]
            for t in range(1, nt):
                psum = psum + p[:, t * LANES:(t + 1) * LANES]
            l_scr[rs] = alpha * l_scr[rs] + psum
            acc_scr[rs] = jnp.concatenate([alpha] * (c // LANES), axis=1) * acc_scr[rs] + _dot(p.astype(BF16), v)
            m_scr[rs] = m_new

        s_cur = scores(0)
        for r in range(nchunk):
            s_next = scores(r + 1) if r + 1 < nchunk else None
            update(r, s_cur)
            s_cur = s_next

    nfull = (i * tq) // tk

    def body(j, carry):
        step(j, False)
        return carry

    lax.fori_loop(0, nfull, body, 0)
    step(nfull, True)

    inv = 1.0 / jnp.sum(l_scr[...], axis=-1, keepdims=True)
    for h in range(heads):
        oh = (acc_scr[h * tq:(h + 1) * tq, :] * inv[h * tq:(h + 1) * tq]).astype(BF16)
        o_ref[:, h * vdim:(h + 1) * vdim] = _dot(oh, wuv_ref[:, h * vdim:(h + 1) * vdim]).astype(BF16)


def _attn_prompt(qc, kc, wuv, bsz, seq, *, heads, vdim, c):
    cw = kc.shape[1]
    tq = ATTN_TQ
    tk = min(ATTN_TK, seq)
    nq = seq // tq
    rows = heads * tq
    nchunk = ATTN_ROW_CHUNKS
    kern = functools.partial(_attn_prompt_kernel, tq=tq, tk=tk, heads=heads, vdim=vdim, c=c, nchunk=nchunk)
    return pl.pallas_call(
        kern,
        grid=(bsz, nq),
        in_specs=[pl.BlockSpec((1, heads, tq, cw), lambda bi, i: (bi * nq + i, 0, 0, 0)),
                  pl.BlockSpec((seq, cw), lambda bi, i: (bi, 0)),
                  _resident(wuv.shape)],
        out_specs=pl.BlockSpec((tq, heads * vdim), lambda bi, i: (bi * nq + i, 0)),
        out_shape=jax.ShapeDtypeStruct((bsz * seq, heads * vdim), BF16),
        scratch_shapes=[pltpu.VMEM((rows, LANES), F32), pltpu.VMEM((rows, LANES), F32),
                        pltpu.VMEM((rows, c), F32)],
        compiler_params=pltpu.CompilerParams(
            dimension_semantics=("parallel", "arbitrary"), vmem_limit_bytes=VMEM_LIMIT),
        name="attn_prompt",
    )(qc, kc, wuv)


def _attn_sample_kernel(pt_ref, q_ref, kn_ref, wuv_ref, cck_ref, cpe_ref, o_ref,
                        bufk, bufp, sem, m_scr, l_scr, acc_scr,
                        *, nseq, npages, ch, nbuf, heads, vdim, ntok, c, rope):
    cps = npages // ch
    total = (nseq // 2) * cps

    def unit_copies(unit, slot):
        pair = unit // cps
        part = unit % cps
        copies = []
        for sq in range(2):
            base = (2 * pair + sq) * npages + part * ch
            for pg in range(ch):
                page = pt_ref[base + pg]
                copies.append(pltpu.make_async_copy(
                    cck_ref.at[page], bufk.at[slot, sq, pl.ds(pg * PAGE_SIZE, PAGE_SIZE), :],
                    sem.at[0, slot]))
                copies.append(pltpu.make_async_copy(
                    cpe_ref.at[page], bufp.at[slot, sq, :, pl.ds(pg * PAGE_SIZE, PAGE_SIZE)],
                    sem.at[1, slot]))
        return copies

    def start_unit(unit):
        for cp in unit_copies(unit, unit % nbuf):
            cp.start()

    def wait_unit(unit):
        for cp in unit_copies(unit, unit % nbuf):
            cp.wait()

    for u0 in range(min(nbuf - 1, total)):
        start_unit(u0)

    def softmax_update(sq, s, v):
        m_prev = m_scr[sq]
        m_new = jnp.maximum(m_prev, jnp.max(s, axis=-1, keepdims=True))
        alpha = jnp.exp2(m_prev - m_new)
        p = jnp.exp2(s - m_new)
        l_scr[sq] = alpha * l_scr[sq] + jnp.sum(p, axis=-1, keepdims=True)
        acc_scr[sq] = alpha * acc_scr[sq] + _dot(p.astype(BF16), v)
        m_scr[sq] = m_new

    def body(u, carry):
        pair = u // cps
        part = u % cps
        slot = u % nbuf

        @pl.when(u + nbuf - 1 < total)
        def _():
            start_unit(u + nbuf - 1)

        wait_unit(u)

        @pl.when(part == 0)
        def _():
            m_scr[...] = jnp.full(m_scr.shape, -jnp.inf, F32)
            l_scr[...] = jnp.zeros(l_scr.shape, F32)
            acc_scr[...] = jnp.zeros(acc_scr.shape, F32)

        kbs, ss = [], []
        for sq in range(2):
            q = q_ref[2 * pair + sq]
            kb = bufk[slot, sq].astype(BF16)
            kpt = bufp[slot, sq].astype(BF16)
            kbs.append(kb)
            ss.append(_dot_nt(q[:, :c], kb) + _dot(q[:, c:c + rope], kpt))
        for sq in range(2):
            softmax_update(sq, ss[sq], kbs[sq])

        @pl.when(part == cps - 1)
        def _():
            for sq in range(2):
                seq = 2 * pair + sq
                q = q_ref[seq]
                kn = kn_ref[seq]
                sn = _dot_nt(q, kn)
                rows, cols = sn.shape
                t_row = _div_pow2(lax.broadcasted_iota(jnp.int32, (rows, 1), 0), heads)
                col = lax.broadcasted_iota(jnp.int32, (1, cols), 1)
                sn = jnp.where((col <= t_row) & (col < ntok), sn, -jnp.inf)
                softmax_update(sq, sn, kn[:, :c])
                o = (acc_scr[sq] / l_scr[sq]).astype(BF16)
                y = _dot(o, wuv_ref[...])
                h_row = lax.broadcasted_iota(jnp.int32, (rows, 1), 0) & (heads - 1)
                h_col = _div_pow2(lax.broadcasted_iota(jnp.int32, (1, heads * vdim), 1), vdim)
                z = jnp.where(h_row == h_col, y, 0.0)
                tid = lax.broadcasted_iota(jnp.int32, (ntok, 1), 0)
                out = jnp.zeros((ntok, heads * vdim), F32)
                for t in range(ntok):
                    zt = jnp.sum(z[t * heads:(t + 1) * heads], axis=0, keepdims=True)
                    out = jnp.where(tid == t, zt, out)
                o_ref[seq] = out

        return carry

    lax.fori_loop(0, total, body, 0)


def _attn_sample(page_table, qs, kn, wuv, cache_ckv, cache_kpet, *, heads, vdim, ntok, c):
    nseq, rows, _ = qs.shape
    npages = page_table.shape[1]
    rope = cache_kpet.shape[1]
    ch = min(SAMPLE_PAGES, npages)
    assert npages % ch == 0 and nseq % 2 == 0
    nbuf = SAMPLE_BUFS
    kern = functools.partial(_attn_sample_kernel, nseq=nseq, npages=npages, ch=ch, nbuf=nbuf,
                             heads=heads, vdim=vdim, ntok=ntok, c=c, rope=rope)
    vmem = pl.BlockSpec(memory_space=pltpu.VMEM)
    return pl.pallas_call(
        kern,
        in_specs=[pl.BlockSpec(memory_space=pltpu.SMEM), vmem, vmem, vmem,
                  pl.BlockSpec(memory_space=pl.ANY), pl.BlockSpec(memory_space=pl.ANY)],
        out_specs=vmem,
        out_shape=jax.ShapeDtypeStruct((nseq, ntok, heads * vdim), F32),
        scratch_shapes=[pltpu.VMEM((nbuf, 2, ch * PAGE_SIZE, c), F32),
                        pltpu.VMEM((nbuf, 2, rope, ch * PAGE_SIZE), F32),
                        pltpu.SemaphoreType.DMA((2, nbuf)),
                        pltpu.VMEM((2, rows, 1), F32), pltpu.VMEM((2, rows, 1), F32),
                        pltpu.VMEM((2, rows, c), F32)],
        compiler_params=pltpu.CompilerParams(vmem_limit_bytes=VMEM_LIMIT),
        name="attn_sample",
    )(page_table.reshape(-1), qs, kn, wuv, cache_ckv, cache_kpet)


def _hgrn_kernel(hv_ref, lf_ref, s0_ref, gn_ref, ob_ref, so_ref,
                 s_scr, q_scr, k_scr, b_scr, at_scr,
                 *, L, sub, n_t, heads, dk, dv):
    c = pl.program_id(1)
    nc = pl.num_programs(1)
    hd = heads * dk
    nsub = L // sub
    levels = int(math.log2(nsub))
    assert 2 ** levels == nsub

    @pl.when(c == 0)
    def _():
        s_scr[...] = s0_ref[0]

    g = lf_ref[...]
    g1 = g.astype(BF16)
    r1 = g - g1.astype(F32)
    g2 = r1.astype(BF16)
    g3 = (r1 - g2.astype(F32)).astype(BF16)
    row = lax.broadcasted_iota(jnp.int32, (L, L), 0)
    col = lax.broadcasted_iota(jnp.int32, (L, L), 1)
    tril = jnp.where(col <= row, 1.0, 0.0).astype(BF16)
    b = _dot(tril, g1) + _dot(tril, g2) + _dot(tril, g3)

    def group_bounds(grp):
        ends = [b[(j + 1) * grp - 1:(j + 1) * grp, :] for j in range(L // grp)]
        starts = [jnp.zeros_like(ends[0])] + ends[:-1]
        rep = lambda rows_: jnp.concatenate([jnp.broadcast_to(r, (grp, hd)) for r in rows_], axis=0)
        return rep(starts), rep(ends)

    bstart, bend = group_bounds(sub)
    brel = b - bstart
    blast = b[L - 1:L, :]

    q = hv_ref[:, 0:hd].astype(F32)
    v = hv_ref[:, hd:2 * hd]
    sg = hv_ref[:, 2 * hd:3 * hd].astype(F32)
    k = hv_ref[:, 3 * hd:4 * hd].astype(F32)

    q_scr[...] = q
    k_scr[...] = k
    b_scr[...] = brel * math.log2(math.e)

    qt = q * jnp.exp(brel)
    kh = k * jnp.exp(bend - b)
    q_in = (qt * jnp.exp(bstart)).astype(BF16)
    k_st = (kh * jnp.exp(blast - bend)).astype(BF16)
    e_last = jnp.exp(blast)

    lhs = [qt.astype(BF16)]
    rhs = [kh.astype(BF16)]
    valid = []
    for lv in range(levels):
        grp = sub * (2 ** lv)
        if lv > 0:
            gs, ge = group_bounds(grp)
            lhs.append((qt * jnp.exp(bstart - gs)).astype(BF16))
            rhs.append((kh * jnp.exp(ge - bend)).astype(BF16))
        cg = _div_pow2(col, grp)
        valid.append(((cg & 1) == 1) & (_div_pow2(row, grp) == cg - 1))

    rid8 = lax.broadcasted_iota(jnp.int32, (8, 1), 0)
    lane = lax.broadcasted_iota(jnp.int32, (8, dk), 1)
    lss = [slice(h * dk, (h + 1) * dk) for h in range(heads)]
    halves = sub // 8

    def diag_body(i, carry):
        r0 = pl.multiple_of(i * sub, sub)
        at = [[jnp.zeros((8, dk), F32) for _ in range(halves)] for _ in range(heads)]
        blk = [[ref[pl.ds(r0, sub), ls] for ref in (q_scr, k_scr, b_scr)] for ls in lss]
        for t in range(n_t):
            is_col = lane == r0 + t
            for h in range(heads):
                qb, kb, bb = blk[h]
                for hf in range(t // 8 + 1):
                    rr = slice(8 * hf, 8 * hf + 8)
                    w = jnp.exp2(bb[t:t + 1, :] - bb[rr]) * kb[rr] * qb[t:t + 1, :]
                    a = jnp.sum(w, axis=-1, keepdims=True)
                    if t < 8 * hf + 7:
                        a = jnp.where(rid8 + 8 * hf <= t, a, 0.0)
                    at[h][hf] = jnp.where(is_col, a, at[h][hf])
        for h, ls in enumerate(lss):
            for hf in range(halves):
                at_scr[pl.ds(pl.multiple_of(r0 + 8 * hf, 8), 8), ls] = at[h][hf]
        return carry

    lax.fori_loop(0, nsub, diag_body, 0)

    gn = gn_ref[...]
    s_prev = [s_scr[h] for h in range(heads)]
    o_in = [_dot(q_in[:, ls], s_prev[h].astype(BF16)) for h, ls in enumerate(lss)]
    p_off = [[_dot_nt(rhs[lv][:, ls], lhs[lv][:, ls]) for lv in range(levels)] for ls in lss]
    s_upd = [_dot_tn(k_st[:, ls], v[:, ls]) for ls in lss]
    o_mix = []
    for h, ls in enumerate(lss):
        if levels > 0:
            a_t = at_scr[:, h * dk:h * dk + L]
            for lv in range(levels):
                a_t = a_t + jnp.where(valid[lv], p_off[h][lv], 0.0)
            o_mix.append(_dot_tn(a_t.astype(BF16), v[:, ls]))
        else:
            o_mix.append(_dot_tn(at_scr[:, ls].astype(BF16), v[:, ls])[:L])
    for h, ls in enumerate(lss):
        decay = jnp.transpose(jnp.broadcast_to(e_last[:, ls], (dk, dk)))
        s_scr[h] = decay * s_prev[h] + s_upd[h]
        o = o_in[h] + o_mix[h]
        on = o * lax.rsqrt(jnp.mean(o * o, axis=-1, keepdims=True) + EPS) * gn
        ob_ref[:, ls] = (on * sg[:, ls]).astype(BF16)

    @pl.when(c == nc - 1)
    def _():
        so_ref[0] = s_scr[...]


def _hgrn(hv, lf, s0, gn, nc, *, L, sub, n_t, heads, dk, dv):
    nseq = s0.shape[0]
    rows = nseq * nc * L
    hd = heads * dk
    kern = functools.partial(_hgrn_kernel, L=L, sub=sub, n_t=n_t, heads=heads, dk=dk, dv=dv)
    return pl.pallas_call(
        kern,
        grid=(nseq, nc),
        in_specs=[pl.BlockSpec((L, 4 * hd), lambda s, c: (s * nc + c, 0)),
                  pl.BlockSpec((L, hd), lambda s, c: (s * nc + c, 0)),
                  pl.BlockSpec((1, heads, dk, dv), lambda s, c: (s, 0, 0, 0)),
                  pl.BlockSpec((1, dv), lambda s, c: (0, 0))],
        out_specs=[pl.BlockSpec((L, heads * dv), lambda s, c: (s * nc + c, 0)),
                   pl.BlockSpec((1, heads, dk, dv), lambda s, c: (s, 0, 0, 0))],
        out_shape=[jax.ShapeDtypeStruct((rows, heads * dv), BF16),
                   jax.ShapeDtypeStruct((nseq, heads, dk, dv), F32)],
        scratch_shapes=[pltpu.VMEM((heads, dk, dv), F32)] + [pltpu.VMEM((L, hd), F32)] * 4,
        compiler_params=pltpu.CompilerParams(
            dimension_semantics=("parallel", "arbitrary"), vmem_limit_bytes=VMEM_LIMIT),
        name="hgrn",
    )(hv, lf, s0, gn)


def _merge_kernel(oa_ref, ob_ref, sg_ref, x_ref, wa_ref, wb_ref, wo_ref, o_ref, *, d):
    a = _dot(oa_ref[...], wa_ref[...])
    b = _dot(ob_ref[...], wb_ref[...])
    merged = sg_ref[:, :d].astype(F32) * a + sg_ref[:, d:].astype(F32) * b
    o_ref[...] = x_ref[...] + _dot(merged.astype(BF16), wo_ref[...])


def _merge(oa, ob, sg, x, wa, wb, wo):
    n, d = x.shape
    tm = _pick_tile([n], (256, 128))
    return pl.pallas_call(
        functools.partial(_merge_kernel, d=d),
        grid=(n // tm,),
        in_specs=[pl.BlockSpec((tm, oa.shape[1]), lambda i: (i, 0)),
                  pl.BlockSpec((tm, ob.shape[1]), lambda i: (i, 0)),
                  pl.BlockSpec((tm, 2 * d), lambda i: (i, 0)),
                  pl.BlockSpec((tm, d), lambda i: (i, 0)),
                  _resident(wa.shape), _resident(wb.shape), _resident(wo.shape)],
        out_specs=pl.BlockSpec((tm, d), lambda i: (i, 0)),
        out_shape=jax.ShapeDtypeStruct((n, d), F32),
        compiler_params=pltpu.CompilerParams(
            dimension_semantics=("parallel",), vmem_limit_bytes=VMEM_LIMIT),
        name="merge_out",
    )(oa, ob, sg, x, wa, wb, wo)


def _ple_kernel(x_ref, pp_ref, ps_ref, g_ref, wg_ref, wp_ref, gf_ref, yp_ref, ys_ref, *, final, npb):
    i = pl.program_id(0)
    x = x_ref[...]
    gate = _sigmoid(_dot(_rms(x, g_ref[...]).astype(BF16), wg_ref[...]))
    p = jnp.where(i < npb, pp_ref[...], ps_ref[...])
    y = x + gate * _dot(p.astype(BF16), wp_ref[...])
    if final:
        y = _rms(y, gf_ref[...])

    @pl.when(i < npb)
    def _():
        yp_ref[...] = y

    @pl.when(i >= npb)
    def _():
        ys_ref[...] = y


def _ple(x, pp, ps, g, wg, wp, gf, final):
    n, d = x.shape
    ptok, stok = pp.shape[0], ps.shape[0]
    tm = _pick_tile([ptok, stok], (256, 128))
    npb = ptok // tm
    return pl.pallas_call(
        functools.partial(_ple_kernel, final=final, npb=npb),
        grid=(n // tm,),
        in_specs=[pl.BlockSpec((tm, d), lambda i: (i, 0))] + _split_specs(tm, pp.shape[1], npb, 1)
                 + [_resident(g.shape), _resident(wg.shape), _resident(wp.shape), _resident(gf.shape)],
        out_specs=_split_specs(tm, d, npb, 1),
        out_shape=[jax.ShapeDtypeStruct((ptok, d), F32), jax.ShapeDtypeStruct((stok, d), F32)],
        compiler_params=pltpu.CompilerParams(
            dimension_semantics=("arbitrary",), vmem_limit_bytes=VMEM_LIMIT),
        name="ple",
    )(x, pp, ps, g, wg, wp, gf)


def _rot_last(w):
    half = w.shape[-1] // 2
    return jnp.concatenate([-w[..., half:], w[..., :half]], axis=-1)


def _pad_to_lanes(w, axis):
    pads = [(0, 0)] * w.ndim
    pads[axis] = (0, LANES - w.shape[axis])
    return jnp.pad(w, pads)


def kernel(x_prompt, x_sample, cache_ckv, cache_kpe, state_hgrn, page_table, p_prompt, p_sample, ffn1_norm, ffn1_w_gate, ffn1_w_up, ffn1_w_down, mix_norm, w_in, q_norm, w_uq, kv_norm, w_uk, w_uv, hgrn_lb_logits, hgrn_out_norm, w_branch_a, w_branch_b, w_out, ffn2_norm, ffn2_w_gate, ffn2_w_up, ffn2_w_down, ple_norm, w_ple_gate, w_ple_proj, final_norm):
    bsz, seq, d = x_prompt.shape
    nseq, ntok, _ = x_sample.shape
    depth = w_in.shape[0]
    q_lora = q_norm.shape[1]
    kv_lora, heads, nope = w_uk.shape[1:]
    vdim = w_uv.shape[-1]
    rope = cache_kpe.shape[-1]
    hb, dk, dv = state_hgrn.shape[2:]
    hd = hb * dk
    npages = page_table.shape[1]
    past_len = npages * PAGE_SIZE
    ptok = bsz * seq
    stok = nseq * ntok
    scale = math.log2(math.e) / math.sqrt(nope + rope)
    assert rope <= LANES and nope == LANES and dk == LANES and dv == LANES

    x = (x_prompt.reshape(ptok, d), x_sample.reshape(stok, d))

    half = rope // 2
    inv = ROPE_THETA ** (-jnp.arange(half, dtype=F32) / half)
    pos = jnp.concatenate([jnp.tile(jnp.arange(seq, dtype=F32), bsz),
                           jnp.tile(jnp.arange(ntok, dtype=F32) + past_len, nseq)])
    ang = pos[:, None] * inv[None, :]
    cos = jnp.cos(ang)
    sin = jnp.sin(ang)
    cs = jnp.concatenate([_pad_to_lanes(jnp.concatenate([cos, cos], axis=1), 1),
                          _pad_to_lanes(jnp.concatenate([sin, sin], axis=1), 1)], axis=1)

    sp = [0]
    for w in (q_lora, kv_lora, rope, hd, hd, hb * dv, hb * dv, d, d):
        sp.append(sp[-1] + w)

    ckv_p, kpe_p, st_p, ckv_s, kpe_s, st_s = [], [], [], [], [], []
    for i in range(depth):
        wit = jnp.swapaxes(w_in[i], 0, 1)
        seg = [wit[sp[j]:sp[j + 1]] for j in range(9)]
        kr_t = seg[2]
        kr_rot_t = jnp.swapaxes(_rot_last(jnp.swapaxes(kr_t, 0, 1)), 0, 1)
        w_mlat = jnp.concatenate([seg[0], seg[1], _pad_to_lanes(kr_t, 0), _pad_to_lanes(kr_rot_t, 0)],
                                 axis=0).astype(BF16)
        w_hgt = jnp.concatenate([seg[3], seg[5], seg[6], seg[4]], axis=0).astype(BF16)
        w_gtt = jnp.concatenate([seg[7], seg[8]], axis=0).astype(BF16)
        wq = w_uq[i].reshape(q_lora, heads, nope + rope)
        wq_rope = wq[:, :, nope:]
        w_uq3 = jnp.concatenate([wq[:, :, :nope].reshape(q_lora, heads * nope),
                                 _pad_to_lanes(wq_rope, 2).reshape(q_lora, heads * LANES),
                                 _pad_to_lanes(_rot_last(wq_rope), 2).reshape(q_lora, heads * LANES)],
                                axis=1).astype(BF16)
        w_ukt = jnp.transpose(w_uk[i], (1, 2, 0)).astype(BF16)
        w_uv2 = w_uv[i].reshape(kv_lora, heads * vdim).astype(BF16)

        x, hmix = _ffn(x, ffn1_norm[i][None], ffn1_w_gate[i].astype(BF16), ffn1_w_up[i].astype(BF16),
                       ffn1_w_down[i].astype(BF16), mix_norm[i][None])

        sg = _gates(hmix, w_gtt)
        hv, lf = _hgrn_proj(hmix, w_hgt, hgrn_lb_logits, i)
        qc, kc, ckv_fp, ckv_fs, kpe_fp, kpe_fs = _mla_prep(
            hmix, w_mlat, q_norm[i][None], kv_norm[i][None], w_uq3, w_ukt, cs, ptok,
            heads=heads, q_lora=q_lora, kv_lora=kv_lora, nope=nope, rope=rope, scale=scale)

        npb = ptok // LANES
        oa_p = _attn_prompt(qc, kc, w_uv2, bsz, seq, heads=heads, vdim=vdim, c=kv_lora)
        cw = qc.shape[-1]
        qs = jnp.transpose(qc[npb:], (0, 2, 1, 3)).reshape(nseq, ntok * heads, cw)
        kn = jnp.pad(kc[ptok:].reshape(nseq, ntok, cw), ((0, 0), (0, 16 - ntok), (0, 0)))
        oa_s = _attn_sample(page_table, qs, kn, w_uv2, cache_ckv[i], jnp.swapaxes(cache_kpe[i], 1, 2),
                            heads=heads, vdim=vdim, ntok=ntok, c=kv_lora)
        oa = jnp.concatenate([oa_p, oa_s.reshape(stok, heads * vdim).astype(BF16)], axis=0)

        gn = hgrn_out_norm[i][None]
        ob_p, s_p = _hgrn(hv, lf, jnp.zeros((bsz, hb, dk, dv), F32), gn, seq // HGRN_CHUNK,
                          L=HGRN_CHUNK, sub=HGRN_SUB, n_t=HGRN_SUB, heads=hb, dk=dk, dv=dv)
        srows = HGRN_SAMPLE_ROWS
        pad = srows - ntok
        hv_s = jnp.pad(hv[ptok:].reshape(nseq, ntok, 4 * hd), ((0, 0), (0, pad), (0, 0)))
        lf_s = jnp.pad(lf[ptok:].reshape(nseq, ntok, hd), ((0, 0), (0, pad), (0, 0)))
        ob_s, s_s = _hgrn(hv_s.reshape(nseq * srows, 4 * hd), lf_s.reshape(nseq * srows, hd),
                          state_hgrn[i], gn, 1, L=srows, sub=srows, n_t=ntok, heads=hb, dk=dk, dv=dv)
        ob = jnp.concatenate([ob_p, ob_s.reshape(nseq, srows, hb * dv)[:, :ntok].reshape(stok, hb * dv)],
                             axis=0)

        x = _merge(oa, ob, sg, x, w_branch_a[i].astype(BF16), w_branch_b[i].astype(BF16),
                   w_out[i].astype(BF16))
        x = _ffn(x, ffn2_norm[i][None], ffn2_w_gate[i].astype(BF16), ffn2_w_up[i].astype(BF16),
                 ffn2_w_down[i].astype(BF16))
        x = tuple(_ple(x, p_prompt[i].reshape(ptok, -1), p_sample[i].reshape(stok, -1), ple_norm[i][None],
                       w_ple_gate[i].astype(BF16), w_ple_proj[i].astype(BF16), final_norm[None],
                       i == depth - 1))

        ckv_p.append(ckv_fp.reshape(bsz, seq, kv_lora))
        kpe_p.append(kpe_fp.reshape(bsz, seq, rope))
        st_p.append(s_p)
        ckv_s.append(ckv_fs.reshape(nseq, ntok, kv_lora))
        kpe_s.append(kpe_fs.reshape(nseq, ntok, rope))
        st_s.append(s_s)

    return (x[0].reshape(bsz, seq, d), x[1].reshape(nseq, ntok, d),
            jnp.stack(ckv_p), jnp.stack(kpe_p), jnp.stack(st_p),
            jnp.stack(ckv_s), jnp.stack(kpe_s), jnp.stack(st_s))
```

```python
import functools
import math

import jax
import jax.numpy as jnp
from jax import lax
from jax.experimental import pallas as pl
from jax.experimental.pallas import tpu as pltpu

F32 = jnp.float32
BF16 = jnp.bfloat16

EPS = 1e-6
ROPE_THETA = 10000.0
PAGE_SIZE = 128
LANES = 128
HGRN_SUB = 16
HGRN_CHUNK = 64
HGRN_SAMPLE_ROWS = 16
HGRN_SAMPLE_GROUP = 4
ATTN_TQ = 128
ATTN_TK = 512
ATTN_ROW_CHUNKS = 4
SAMPLE_PAGES = 8
SAMPLE_BUFS = 4
VMEM_LIMIT = 60 * 1024 * 1024


def _sigmoid(x):
    return 1.0 / (1.0 + jnp.exp(-x))


def _rms(x, g):
    return x * lax.rsqrt(jnp.mean(x * x, axis=-1, keepdims=True) + EPS) * g


def _dot(a, b):
    return jnp.dot(a, b, preferred_element_type=F32)


def _dot_nt(a, b):
    return lax.dot_general(a, b, (((1,), (1,)), ((), ())), preferred_element_type=F32)


def _dot_tn(a, b):
    return lax.dot_general(a, b, (((0,), (0,)), ((), ())), preferred_element_type=F32)


def _div_pow2(x, n):
    shift = int(math.log2(n))
    assert 2 ** shift == n
    return x >> shift


def _pick_tile(sizes, candidates):
    for c in candidates:
        if all(n % c == 0 for n in sizes):
            return c
    raise ValueError(f"no tile in {candidates} divides {sizes}")


def _resident(shape):
    nd = len(shape)
    return pl.BlockSpec(shape, lambda *_: (0,) * nd, pipeline_mode=pl.Buffered(1))


def _split_specs(tm, width, npb, ngrid):
    if ngrid == 1:
        return [pl.BlockSpec((tm, width), lambda i: (jnp.minimum(i, npb - 1), 0)),
                pl.BlockSpec((tm, width), lambda i: (jnp.maximum(i - npb, 0), 0))]
    return [pl.BlockSpec((tm, width), lambda i, j: (jnp.minimum(i, npb - 1), 0)),
            pl.BlockSpec((tm, width), lambda i, j: (jnp.maximum(i - npb, 0), 0))]


def _ffn_kernel(*refs, nf, post, npb):
    refs = list(refs)
    split = npb is not None
    xp_ref = refs.pop(0)
    xs_ref = refs.pop(0) if split else None
    g_ref, wg_ref, wu_ref, wd_ref = refs[:4]
    refs = refs[4:]
    g2_ref = refs.pop(0) if post else None
    o_ref = refs.pop(0)
    h2_ref = refs.pop(0) if post else None
    h_scr, acc_scr = refs[:2]
    x_scr = refs[2] if split else xp_ref
    i = pl.program_id(0)
    f = pl.program_id(1)

    @pl.when(f == 0)
    def _():
        if split:
            @pl.when(i < npb)
            def _():
                x_scr[...] = xp_ref[...]

            @pl.when(i >= npb)
            def _():
                x_scr[...] = xs_ref[...]
        h_scr[...] = _rms(x_scr[...], g_ref[...]).astype(BF16)
        acc_scr[...] = jnp.zeros(acc_scr.shape, F32)

    h = h_scr[...]
    g = _dot(h, wg_ref[...])
    u = _dot(h, wu_ref[...])
    a = (g * _sigmoid(g) * u).astype(BF16)
    acc_scr[...] += _dot(a, wd_ref[...])

    @pl.when(f == nf - 1)
    def _():
        y = x_scr[...] + 0.5 * acc_scr[...]
        o_ref[...] = y
        if post:
            h2_ref[...] = _rms(y, g2_ref[...]).astype(BF16)


def _ffn(x, g, wg, wu, wd, g2=None):
    split = isinstance(x, tuple)
    parts = x if split else (x,)
    d = parts[0].shape[1]
    n = sum(p.shape[0] for p in parts)
    dff = wg.shape[1]
    tm = _pick_tile([p.shape[0] for p in parts], (512, 256, 128))
    tf = _pick_tile([dff], (512, 256, 128))
    nf = dff // tf
    npb = parts[0].shape[0] // tm if split else None
    post = g2 is not None
    if split:
        in_specs = _split_specs(tm, d, npb, 2)
    else:
        in_specs = [pl.BlockSpec((tm, d), lambda i, f: (i, 0))]
    in_specs += [
        pl.BlockSpec((1, d), lambda i, f: (0, 0)),
        pl.BlockSpec((d, tf), lambda i, f: (0, f)),
        pl.BlockSpec((d, tf), lambda i, f: (0, f)),
        pl.BlockSpec((tf, d), lambda i, f: (f, 0)),
    ]
    args = list(parts) + [g, wg, wu, wd]
    out_shape = [jax.ShapeDtypeStruct((n, d), F32)]
    out_specs = [pl.BlockSpec((tm, d), lambda i, f: (i, 0))]
    if post:
        in_specs.append(pl.BlockSpec((1, d), lambda i, f: (0, 0)))
        args.append(g2)
        out_shape.append(jax.ShapeDtypeStruct((n, d), BF16))
        out_specs.append(pl.BlockSpec((tm, d), lambda i, f: (i, 0)))
    scratch = [pltpu.VMEM((tm, d), BF16), pltpu.VMEM((tm, d), F32)]
    if split:
        scratch.append(pltpu.VMEM((tm, d), F32))
    res = pl.pallas_call(
        functools.partial(_ffn_kernel, nf=nf, post=post, npb=npb),
        grid=(n // tm, nf),
        in_specs=in_specs,
        out_specs=out_specs,
        out_shape=out_shape,
        scratch_shapes=scratch,
        compiler_params=pltpu.CompilerParams(
            dimension_semantics=("parallel", "arbitrary"), vmem_limit_bytes=VMEM_LIMIT),
        name="ffn_post" if post else "ffn",
    )(*args)
    return res if post else res[0]


def _gates_kernel(h_ref, w_ref, o_ref):
    o_ref[...] = _sigmoid(_dot_nt(h_ref[...], w_ref[...])).astype(BF16)


def _gates(h, wt):
    n, d = h.shape
    nc = wt.shape[0]
    tm = _pick_tile([n], (512, 256, 128))
    tn = _pick_tile([nc], (2048, 1024, 512, 256, 128))
    return pl.pallas_call(
        _gates_kernel,
        grid=(n // tm, nc // tn),
        in_specs=[pl.BlockSpec((tm, d), lambda i, j: (i, 0)),
                  pl.BlockSpec((tn, d), lambda i, j: (j, 0))],
        out_specs=pl.BlockSpec((tm, tn), lambda i, j: (i, j)),
        out_shape=jax.ShapeDtypeStruct((n, nc), BF16),
        compiler_params=pltpu.CompilerParams(
            dimension_semantics=("parallel", "arbitrary"), vmem_limit_bytes=VMEM_LIMIT),
        name="gates",
    )(h, wt)


def _hgrn_proj_kernel(h_ref, w_ref, lbl_ref, hv_ref, lf_ref, *, layer):
    j = pl.program_id(1)

    def proj():
        return _dot_nt(h_ref[...], w_ref[...])

    @pl.when(j < 2)
    def _():
        hv_ref[...] = proj().astype(BF16)

    @pl.when(j == 2)
    def _():
        y = proj()
        hv_ref[...] = (y * _sigmoid(y)).astype(BF16)

    @pl.when(j == 3)
    def _():
        y = proj()
        logits = lbl_ref[...]
        e = jnp.exp(logits - jnp.max(logits, axis=0, keepdims=True))
        lb = jnp.sum(e[:layer + 1], axis=0, keepdims=True) / jnp.sum(e, axis=0, keepdims=True)
        lf_ref[...] = jnp.log(lb + (1.0 - lb) * _sigmoid(y))
        hv_ref[...] = ((1.0 - lb) * _sigmoid(-y)).astype(BF16)


def _hgrn_proj(h, wt, lb_logits, layer):
    n, d = h.shape
    c = wt.shape[0] // 4
    tm = _pick_tile([n], (512, 256, 128))
    return pl.pallas_call(
        functools.partial(_hgrn_proj_kernel, layer=layer),
        grid=(n // tm, 4),
        in_specs=[pl.BlockSpec((tm, d), lambda i, j: (i, 0)),
                  pl.BlockSpec((c, d), lambda i, j: (j, 0)),
                  pl.BlockSpec(lb_logits.shape, lambda i, j: (0, 0))],
        out_specs=[pl.BlockSpec((tm, c), lambda i, j: (i, j)),
                   pl.BlockSpec((tm, c), lambda i, j: (i, 0))],
        out_shape=[jax.ShapeDtypeStruct((n, 4 * c), BF16),
                   jax.ShapeDtypeStruct((n, c), F32)],
        compiler_params=pltpu.CompilerParams(
            dimension_semantics=("parallel", "arbitrary"), vmem_limit_bytes=VMEM_LIMIT),
        name="hgrn_proj",
    )(h, wt, lb_logits)


def _mla_prep_kernel(h_ref, wm_ref, qn_ref, kvn_ref, wuq_ref, wuk_ref, cs_ref,
                     qc_ref, kc_ref, ckvp_ref, ckvs_ref, kpep_ref, kpes_ref,
                     *, heads, q_lora, kv_lora, nope, rope, scale, npb):
    i = pl.program_id(0)
    tm = h_ref.shape[0]
    c = _dot_nt(h_ref[...], wm_ref[...])
    cos = cs_ref[:, :LANES]
    sin = cs_ref[:, LANES:]
    ckv = _rms(c[:, q_lora:q_lora + kv_lora], kvn_ref[...])
    o = q_lora + kv_lora
    kpe = c[:, o:o + LANES] * cos + c[:, o + LANES:o + 2 * LANES] * sin
    kc_ref[:, :kv_lora] = ckv.astype(BF16)
    kc_ref[:, kv_lora:] = kpe.astype(BF16)

    @pl.when(i < npb)
    def _():
        ckvp_ref[...] = ckv
        kpep_ref[...] = kpe[:, :rope]

    @pl.when(i >= npb)
    def _():
        ckvs_ref[...] = ckv
        kpes_ref[...] = kpe[:, :rope]

    qn = _rms(c[:, :q_lora], qn_ref[...]).astype(BF16)
    q3 = _dot(qn, wuq_ref[...]) * scale
    nb = tm // LANES
    for h in range(heads):
        qh = q3[:, h * nope:(h + 1) * nope].astype(BF16)
        lat = _dot(qh, wuk_ref[h]).astype(BF16)
        qc_ref[:, h, :, :kv_lora] = lat.reshape(nb, LANES, kv_lora)
        r0 = heads * nope + h * LANES
        r1 = heads * nope + heads * LANES + h * LANES
        pe = (q3[:, r0:r0 + LANES] * cos + q3[:, r1:r1 + LANES] * sin).astype(BF16)
        qc_ref[:, h, :, kv_lora:] = pe.reshape(nb, LANES, LANES)


def _mla_prep(h, w_mlat, q_norm, kv_norm, w_uq3, w_ukt, cs, ptok, *, heads, q_lora, kv_lora, nope, rope, scale):
    n, d = h.shape
    stok = n - ptok
    tm = _pick_tile([ptok, stok], (512, 256, 128))
    nb = tm // LANES
    npb = ptok // tm
    cw = kv_lora + LANES
    kern = functools.partial(_mla_prep_kernel, heads=heads, q_lora=q_lora, kv_lora=kv_lora,
                             nope=nope, rope=rope, scale=scale, npb=npb)
    return pl.pallas_call(
        kern,
        grid=(n // tm,),
        in_specs=[pl.BlockSpec((tm, d), lambda i: (i, 0)),
                  _resident(w_mlat.shape), _resident(q_norm.shape), _resident(kv_norm.shape),
                  _resident(w_uq3.shape), _resident(w_ukt.shape),
                  pl.BlockSpec((tm, 2 * LANES), lambda i: (i, 0))],
        out_specs=[pl.BlockSpec((nb, heads, LANES, cw), lambda i: (i, 0, 0, 0)),
                   pl.BlockSpec((tm, cw), lambda i: (i, 0))]
                  + _split_specs(tm, kv_lora, npb, 1) + _split_specs(tm, rope, npb, 1),
        out_shape=[jax.ShapeDtypeStruct((n // LANES, heads, LANES, cw), BF16),
                   jax.ShapeDtypeStruct((n, cw), BF16),
                   jax.ShapeDtypeStruct((ptok, kv_lora), F32),
                   jax.ShapeDtypeStruct((stok, kv_lora), F32),
                   jax.ShapeDtypeStruct((ptok, rope), F32),
                   jax.ShapeDtypeStruct((stok, rope), F32)],
        compiler_params=pltpu.CompilerParams(
            dimension_semantics=("arbitrary",), vmem_limit_bytes=VMEM_LIMIT),
        name="mla_prep",
    )(h, w_mlat, q_norm, kv_norm, w_uq3, w_ukt, cs)


def _attn_prompt_kernel(q_ref, k_ref, wuv_ref, o_ref, m_scr, l_scr, acc_scr,
                        *, tq, tk, heads, vdim, c, nchunk):
    i = pl.program_id(1)
    hpc = heads // nchunk
    rc = hpc * tq
    m_scr[...] = jnp.full(m_scr.shape, -jnp.inf, F32)
    l_scr[...] = jnp.zeros(l_scr.shape, F32)
    acc_scr[...] = jnp.zeros(acc_scr.shape, F32)

    nt = tk // LANES

    def step(j, masked):
        off = pl.multiple_of(j * tk, tk)
        k = k_ref[pl.ds(off, tk), :]
        v = k[:, :c]
        if masked:
            tok = i * tq + (lax.broadcasted_iota(jnp.int32, (rc, 1), 0) & (tq - 1))
            col = off + lax.broadcasted_iota(jnp.int32, (1, tk), 1)
            keep = col <= tok

        def scores(r):
            q = q_ref[0, r * hpc:(r + 1) * hpc].reshape(rc, q_ref.shape[-1])
            s = _dot_nt(q, k)
            return jnp.where(keep, s, -jnp.inf) if masked else s

        def update(r, s):
            rs = slice(r * rc, (r + 1) * rc)
            m_prev = m_scr[rs]
            smax = s[:, :LANES]
            for t in range(1, nt):
                smax = jnp.maximum(smax, s[:, t * LANES:(t + 1) * LANES])
            m_new = jnp.maximum(m_prev, jnp.max(smax, axis=-1, keepdims=True))
            alpha = jnp.exp2(m_prev - m_new)
            p = jnp.exp2(s - jnp.concatenate([m_new] * nt, axis=1))
            psum = p[:, :LANES]
            for t in range(1, nt):
                psum = psum + p[:, t * LANES:(t + 1) * LANES]
            l_scr[rs] = alpha * l_scr[rs] + psum
            acc_scr[rs] = jnp.concatenate([alpha] * (c // LANES), axis=1) * acc_scr[rs] + _dot(p.astype(BF16), v)
            m_scr[rs] = m_new

        s_cur = scores(0)
        for r in range(nchunk):
            s_next = scores(r + 1) if r + 1 < nchunk else None
            update(r, s_cur)
            s_cur = s_next

    nfull = (i * tq) // tk

    def body(j, carry):
        step(j, False)
        return carry

    lax.fori_loop(0, nfull, body, 0)
    step(nfull, True)

    inv = 1.0 / jnp.sum(l_scr[...], axis=-1, keepdims=True)
    for h in range(heads):
        oh = (acc_scr[h * tq:(h + 1) * tq, :] * inv[h * tq:(h + 1) * tq]).astype(BF16)
        o_ref[:, h * vdim:(h + 1) * vdim] = _dot(oh, wuv_ref[:, h * vdim:(h + 1) * vdim]).astype(BF16)


def _attn_prompt(qc, kc, wuv, bsz, seq, *, heads, vdim, c):
    cw = kc.shape[1]
    tq = ATTN_TQ
    tk = min(ATTN_TK, seq)
    nq = seq // tq
    rows = heads * tq
    nchunk = ATTN_ROW_CHUNKS
    kern = functools.partial(_attn_prompt_kernel, tq=tq, tk=tk, heads=heads, vdim=vdim, c=c, nchunk=nchunk)
    return pl.pallas_call(
        kern,
        grid=(bsz, nq),
        in_specs=[pl.BlockSpec((1, heads, tq, cw), lambda bi, i: (bi * nq + i, 0, 0, 0)),
                  pl.BlockSpec((seq, cw), lambda bi, i: (bi, 0)),
                  _resident(wuv.shape)],
        out_specs=pl.BlockSpec((tq, heads * vdim), lambda bi, i: (bi * nq + i, 0)),
        out_shape=jax.ShapeDtypeStruct((bsz * seq, heads * vdim), BF16),
        scratch_shapes=[pltpu.VMEM((rows, LANES), F32), pltpu.VMEM((rows, LANES), F32),
                        pltpu.VMEM((rows, c), F32)],
        compiler_params=pltpu.CompilerParams(
            dimension_semantics=("parallel", "arbitrary"), vmem_limit_bytes=VMEM_LIMIT),
        name="attn_prompt",
    )(qc, kc, wuv)


def _attn_sample_kernel(pt_ref, q_ref, kn_ref, wuv_ref, cck_ref, cpe_ref, o_ref,
                        bufk, bufp, sem, m_scr, l_scr, acc_scr,
                        *, nseq, npages, ch, nbuf, heads, vdim, ntok, c, rope):
    cps = npages // ch
    total = (nseq // 2) * cps

    def unit_copies(unit, slot):
        pair = unit // cps
        part = unit % cps
        copies = []
        for sq in range(2):
            base = (2 * pair + sq) * npages + part * ch
            for pg in range(ch):
                page = pt_ref[base + pg]
                copies.append(pltpu.make_async_copy(
                    cck_ref.at[page], bufk.at[slot, sq, pl.ds(pg * PAGE_SIZE, PAGE_SIZE), :],
                    sem.at[0, slot]))
                copies.append(pltpu.make_async_copy(
                    cpe_ref.at[page], bufp.at[slot, sq, :, pl.ds(pg * PAGE_SIZE, PAGE_SIZE)],
                    sem.at[1, slot]))
        return copies

    def start_unit(unit):
        for cp in unit_copies(unit, unit % nbuf):
            cp.start()

    def wait_unit(unit):
        for cp in unit_copies(unit, unit % nbuf):
            cp.wait()

    for u0 in range(min(nbuf - 1, total)):
        start_unit(u0)

    def softmax_update(sq, s, v):
        m_prev = m_scr[sq]
        m_new = jnp.maximum(m_prev, jnp.max(s, axis=-1, keepdims=True))
        alpha = jnp.exp2(m_prev - m_new)
        p = jnp.exp2(s - m_new)
        l_scr[sq] = alpha * l_scr[sq] + jnp.sum(p, axis=-1, keepdims=True)
        acc_scr[sq] = alpha * acc_scr[sq] + _dot(p.astype(BF16), v)
        m_scr[sq] = m_new

    def body(u, carry):
        pair = u // cps
        part = u % cps
        slot = u % nbuf

        @pl.when(u + nbuf - 1 < total)
        def _():
            start_unit(u + nbuf - 1)

        wait_unit(u)

        @pl.when(part == 0)
        def _():
            m_scr[...] = jnp.full(m_scr.shape, -jnp.inf, F32)
            l_scr[...] = jnp.zeros(l_scr.shape, F32)
            acc_scr[...] = jnp.zeros(acc_scr.shape, F32)

        kbs, ss = [], []
        for sq in range(2):
            q = q_ref[2 * pair + sq]
            kb = bufk[slot, sq].astype(BF16)
            kpt = bufp[slot, sq].astype(BF16)
            kbs.append(kb)
            ss.append(_dot_nt(q[:, :c], kb) + _dot(q[:, c:c + rope], kpt))
        for sq in range(2):
            softmax_update(sq, ss[sq], kbs[sq])

        @pl.when(part == cps - 1)
        def _():
            for sq in range(2):
                seq = 2 * pair + sq
                q = q_ref[seq]
                kn = kn_ref[seq]
                sn = _dot_nt(q, kn)
                rows, cols = sn.shape
                t_row = _div_pow2(lax.broadcasted_iota(jnp.int32, (rows, 1), 0), heads)
                col = lax.broadcasted_iota(jnp.int32, (1, cols), 1)
                sn = jnp.where((col <= t_row) & (col < ntok), sn, -jnp.inf)
                softmax_update(sq, sn, kn[:, :c])
                o = (acc_scr[sq] / l_scr[sq]).astype(BF16)
                y = _dot(o, wuv_ref[...])
                h_row = lax.broadcasted_iota(jnp.int32, (rows, 1), 0) & (heads - 1)
                h_col = _div_pow2(lax.broadcasted_iota(jnp.int32, (1, heads * vdim), 1), vdim)
                z = jnp.where(h_row == h_col, y, 0.0)
                tid = lax.broadcasted_iota(jnp.int32, (ntok, 1), 0)
                out = jnp.zeros((ntok, heads * vdim), F32)
                for t in range(ntok):
                    zt = jnp.sum(z[t * heads:(t + 1) * heads], axis=0, keepdims=True)
                    out = jnp.where(tid == t, zt, out)
                o_ref[seq] = out

        return carry

    lax.fori_loop(0, total, body, 0)


def _attn_sample(page_table, qs, kn, wuv, cache_ckv, cache_kpet, *, heads, vdim, ntok, c):
    nseq, rows, _ = qs.shape
    npages = page_table.shape[1]
    rope = cache_kpet.shape[1]
    ch = min(SAMPLE_PAGES, npages)
    assert npages % ch == 0 and nseq % 2 == 0
    nbuf = SAMPLE_BUFS
    kern = functools.partial(_attn_sample_kernel, nseq=nseq, npages=npages, ch=ch, nbuf=nbuf,
                             heads=heads, vdim=vdim, ntok=ntok, c=c, rope=rope)
    vmem = pl.BlockSpec(memory_space=pltpu.VMEM)
    return pl.pallas_call(
        kern,
        in_specs=[pl.BlockSpec(memory_space=pltpu.SMEM), vmem, vmem, vmem,
                  pl.BlockSpec(memory_space=pl.ANY), pl.BlockSpec(memory_space=pl.ANY)],
        out_specs=vmem,
        out_shape=jax.ShapeDtypeStruct((nseq, ntok, heads * vdim), F32),
        scratch_shapes=[pltpu.VMEM((nbuf, 2, ch * PAGE_SIZE, c), F32),
                        pltpu.VMEM((nbuf, 2, rope, ch * PAGE_SIZE), F32),
                        pltpu.SemaphoreType.DMA((2, nbuf)),
                        pltpu.VMEM((2, rows, 1), F32), pltpu.VMEM((2, rows, 1), F32),
                        pltpu.VMEM((2, rows, c), F32)],
        compiler_params=pltpu.CompilerParams(vmem_limit_bytes=VMEM_LIMIT),
        name="attn_sample",
    )(page_table.reshape(-1), qs, kn, wuv, cache_ckv, cache_kpet)


def _hgrn_kernel(hv_ref, lf_ref, s0_ref, gn_ref, ob_ref, so_ref,
                 s_scr, q_scr, k_scr, b_scr, at_scr,
                 *, L, G, sub, n_t, heads, dk, dv):
    c = pl.program_id(1)
    nc = pl.num_programs(1)

    @pl.when(c == 0)
    def _():
        s_scr[...] = s0_ref[...]

    for gi in range(G):
        _hgrn_chunk(hv_ref, lf_ref, gn_ref, ob_ref, s_scr, q_scr, k_scr, b_scr, at_scr,
                    gi, L=L, sub=sub, n_t=n_t, heads=heads, dk=dk, dv=dv)

    @pl.when(c == nc - 1)
    def _():
        so_ref[...] = s_scr[...]


def _hgrn_chunk(hv_ref, lf_ref, gn_ref, ob_ref, s_scr, q_scr, k_scr, b_scr, at_scr,
                gi, *, L, sub, n_t, heads, dk, dv):
    hd = heads * dk
    nsub = L // sub
    levels = int(math.log2(nsub))
    assert 2 ** levels == nsub
    rows = slice(gi * L, (gi + 1) * L)

    g = lf_ref[rows, :]
    g1 = g.astype(BF16)
    r1 = g - g1.astype(F32)
    g2 = r1.astype(BF16)
    g3 = (r1 - g2.astype(F32)).astype(BF16)
    row = lax.broadcasted_iota(jnp.int32, (L, L), 0)
    col = lax.broadcasted_iota(jnp.int32, (L, L), 1)
    tril = jnp.where(col <= row, 1.0, 0.0).astype(BF16)
    b = _dot(tril, g1) + _dot(tril, g2) + _dot(tril, g3)

    def group_bounds(grp):
        ends = [b[(j + 1) * grp - 1:(j + 1) * grp, :] for j in range(L // grp)]
        starts = [jnp.zeros_like(ends[0])] + ends[:-1]
        rep = lambda rows_: jnp.concatenate([jnp.broadcast_to(r, (grp, hd)) for r in rows_], axis=0)
        return rep(starts), rep(ends)

    bstart, bend = group_bounds(sub)
    brel = b - bstart
    blast = b[L - 1:L, :]

    q = hv_ref[rows, 0:hd].astype(F32)
    v = hv_ref[rows, hd:2 * hd]
    sg = hv_ref[rows, 2 * hd:3 * hd].astype(F32)
    k = hv_ref[rows, 3 * hd:4 * hd].astype(F32)

    q_scr[rows, :] = q
    k_scr[rows, :] = k
    b_scr[rows, :] = brel * math.log2(math.e)

    qt = q * jnp.exp(brel)
    kh = k * jnp.exp(bend - b)
    q_in = (qt * jnp.exp(bstart)).astype(BF16)
    k_st = (kh * jnp.exp(blast - bend)).astype(BF16)
    e_last = jnp.exp(blast)

    lhs = [qt.astype(BF16)]
    rhs = [kh.astype(BF16)]
    valid = []
    for lv in range(levels):
        grp = sub * (2 ** lv)
        if lv > 0:
            gs, ge = group_bounds(grp)
            lhs.append((qt * jnp.exp(bstart - gs)).astype(BF16))
            rhs.append((kh * jnp.exp(ge - bend)).astype(BF16))
        cg = _div_pow2(col, grp)
        valid.append(((cg & 1) == 1) & (_div_pow2(row, grp) == cg - 1))

    rid8 = lax.broadcasted_iota(jnp.int32, (8, 1), 0)
    lane = lax.broadcasted_iota(jnp.int32, (8, dk), 1)
    lss = [slice(h * dk, (h + 1) * dk) for h in range(heads)]
    halves = sub // 8

    def diag_body(i, carry):
        r0 = pl.multiple_of(gi * L + i * sub, sub)
        at = [[jnp.zeros((8, dk), F32) for _ in range(halves)] for _ in range(heads)]
        blk = [[ref[pl.ds(r0, sub), ls] for ref in (q_scr, k_scr, b_scr)] for ls in lss]
        for t in range(n_t):
            is_col = lane == i * sub + t
            for h in range(heads):
                qb, kb, bb = blk[h]
                for hf in range(t // 8 + 1):
                    rr = slice(8 * hf, 8 * hf + 8)
                    w = jnp.exp2(bb[t:t + 1, :] - bb[rr]) * kb[rr] * qb[t:t + 1, :]
                    a = jnp.sum(w, axis=-1, keepdims=True)
                    if t < 8 * hf + 7:
                        a = jnp.where(rid8 + 8 * hf <= t, a, 0.0)
                    at[h][hf] = jnp.where(is_col, a, at[h][hf])
        for h, ls in enumerate(lss):
            for hf in range(halves):
                at_scr[pl.ds(pl.multiple_of(r0 + 8 * hf, 8), 8), ls] = at[h][hf]
        return carry

    lax.fori_loop(0, nsub, diag_body, 0)

    gn = gn_ref[...]
    s_prev = [s_scr[gi, h] for h in range(heads)]
    o_in = [_dot(q_in[:, ls], s_prev[h].astype(BF16)) for h, ls in enumerate(lss)]
    p_off = [[_dot_nt(rhs[lv][:, ls], lhs[lv][:, ls]) for lv in range(levels)] for ls in lss]
    s_upd = [_dot_tn(k_st[:, ls], v[:, ls]) for ls in lss]
    o_mix = []
    for h, ls in enumerate(lss):
        if levels > 0:
            a_t = at_scr[rows, h * dk:h * dk + L]
            for lv in range(levels):
                a_t = a_t + jnp.where(valid[lv], p_off[h][lv], 0.0)
            o_mix.append(_dot_tn(a_t.astype(BF16), v[:, ls]))
        else:
            o_mix.append(_dot_tn(at_scr[rows, ls].astype(BF16), v[:, ls])[:L])
    for h, ls in enumerate(lss):
        decay = jnp.transpose(jnp.broadcast_to(e_last[:, ls], (dk, dk)))
        s_scr[gi, h] = decay * s_prev[h] + s_upd[h]
        o = o_in[h] + o_mix[h]
        on = o * lax.rsqrt(jnp.mean(o * o, axis=-1, keepdims=True) + EPS) * gn
        ob_ref[rows, ls] = (on * sg[:, ls]).astype(BF16)


def _hgrn(hv, lf, s0, gn, nc, *, L, G, sub, n_t, heads, dk, dv):
    nseq = s0.shape[0]
    rows = nseq * nc * L
    hd = heads * dk
    assert nseq % G == 0 and (G == 1 or nc == 1)
    kern = functools.partial(_hgrn_kernel, L=L, G=G, sub=sub, n_t=n_t, heads=heads, dk=dk, dv=dv)
    return pl.pallas_call(
        kern,
        grid=(nseq // G, nc),
        in_specs=[pl.BlockSpec((G * L, 4 * hd), lambda s, c: (s * nc + c, 0)),
                  pl.BlockSpec((G * L, hd), lambda s, c: (s * nc + c, 0)),
                  pl.BlockSpec((G, heads, dk, dv), lambda s, c: (s, 0, 0, 0)),
                  pl.BlockSpec((1, dv), lambda s, c: (0, 0))],
        out_specs=[pl.BlockSpec((G * L, heads * dv), lambda s, c: (s * nc + c, 0)),
                   pl.BlockSpec((G, heads, dk, dv), lambda s, c: (s, 0, 0, 0))],
        out_shape=[jax.ShapeDtypeStruct((rows, heads * dv), BF16),
                   jax.ShapeDtypeStruct((nseq, heads, dk, dv), F32)],
        scratch_shapes=[pltpu.VMEM((G, heads, dk, dv), F32)] + [pltpu.VMEM((G * L, hd), F32)] * 4,
        compiler_params=pltpu.CompilerParams(
            dimension_semantics=("parallel", "arbitrary"), vmem_limit_bytes=VMEM_LIMIT),
        name="hgrn",
    )(hv, lf, s0, gn)


def _merge_kernel(oap_ref, oas_ref, obp_ref, obs_ref, sg_ref, x_ref, wa_ref, wb_ref, wo_ref, o_ref, *, d, npb):
    prompt = pl.program_id(0) < npb
    a = _dot(jnp.where(prompt, oap_ref[...], oas_ref[...]), wa_ref[...])
    b = _dot(jnp.where(prompt, obp_ref[...], obs_ref[...]), wb_ref[...])
    merged = sg_ref[:, :d].astype(F32) * a + sg_ref[:, d:].astype(F32) * b
    o_ref[...] = x_ref[...] + _dot(merged.astype(BF16), wo_ref[...])


def _merge(oa, ob, sg, x, wa, wb, wo):
    n, d = x.shape
    ptok, stok = oa[0].shape[0], oa[1].shape[0]
    tm = _pick_tile([ptok, stok], (512, 256, 128))
    npb = ptok // tm
    return pl.pallas_call(
        functools.partial(_merge_kernel, d=d, npb=npb),
        grid=(n // tm,),
        in_specs=_split_specs(tm, oa[0].shape[1], npb, 1) + _split_specs(tm, ob[0].shape[1], npb, 1)
                 + [pl.BlockSpec((tm, 2 * d), lambda i: (i, 0)),
                    pl.BlockSpec((tm, d), lambda i: (i, 0)),
                    _resident(wa.shape), _resident(wb.shape), _resident(wo.shape)],
        out_specs=pl.BlockSpec((tm, d), lambda i: (i, 0)),
        out_shape=jax.ShapeDtypeStruct((n, d), F32),
        compiler_params=pltpu.CompilerParams(
            dimension_semantics=("parallel",), vmem_limit_bytes=VMEM_LIMIT),
        name="merge_out",
    )(oa[0], oa[1], ob[0], ob[1], sg, x, wa, wb, wo)


def _ple_kernel(x_ref, pp_ref, ps_ref, g_ref, wg_ref, wp_ref, gf_ref, yp_ref, ys_ref, *, final, npb):
    i = pl.program_id(0)
    x = x_ref[...]
    gate = _sigmoid(_dot(_rms(x, g_ref[...]).astype(BF16), wg_ref[...]))
    p = jnp.where(i < npb, pp_ref[...], ps_ref[...])
    y = x + gate * _dot(p.astype(BF16), wp_ref[...])
    if final:
        y = _rms(y, gf_ref[...])

    @pl.when(i < npb)
    def _():
        yp_ref[...] = y

    @pl.when(i >= npb)
    def _():
        ys_ref[...] = y


def _ple(x, pp, ps, g, wg, wp, gf, final):
    n, d = x.shape
    ptok, stok = pp.shape[0], ps.shape[0]
    tm = _pick_tile([ptok, stok], (512, 256, 128))
    npb = ptok // tm
    return pl.pallas_call(
        functools.partial(_ple_kernel, final=final, npb=npb),
        grid=(n // tm,),
        in_specs=[pl.BlockSpec((tm, d), lambda i: (i, 0))] + _split_specs(tm, pp.shape[1], npb, 1)
                 + [_resident(g.shape), _resident(wg.shape), _resident(wp.shape), _resident(gf.shape)],
        out_specs=_split_specs(tm, d, npb, 1),
        out_shape=[jax.ShapeDtypeStruct((ptok, d), F32), jax.ShapeDtypeStruct((stok, d), F32)],
        compiler_params=pltpu.CompilerParams(
            dimension_semantics=("arbitrary",), vmem_limit_bytes=VMEM_LIMIT),
        name="ple",
    )(x, pp, ps, g, wg, wp, gf)


def _rot_last(w):
    half = w.shape[-1] // 2
    return jnp.concatenate([-w[..., half:], w[..., :half]], axis=-1)


def _pad_to_lanes(w, axis):
    pads = [(0, 0)] * w.ndim
    pads[axis] = (0, LANES - w.shape[axis])
    return jnp.pad(w, pads)


def kernel(x_prompt, x_sample, cache_ckv, cache_kpe, state_hgrn, page_table, p_prompt, p_sample, ffn1_norm, ffn1_w_gate, ffn1_w_up, ffn1_w_down, mix_norm, w_in, q_norm, w_uq, kv_norm, w_uk, w_uv, hgrn_lb_logits, hgrn_out_norm, w_branch_a, w_branch_b, w_out, ffn2_norm, ffn2_w_gate, ffn2_w_up, ffn2_w_down, ple_norm, w_ple_gate, w_ple_proj, final_norm):
    bsz, seq, d = x_prompt.shape
    nseq, ntok, _ = x_sample.shape
    depth = w_in.shape[0]
    q_lora = q_norm.shape[1]
    kv_lora, heads, nope = w_uk.shape[1:]
    vdim = w_uv.shape[-1]
    rope = cache_kpe.shape[-1]
    hb, dk, dv = state_hgrn.shape[2:]
    hd = hb * dk
    npages = page_table.shape[1]
    past_len = npages * PAGE_SIZE
    ptok = bsz * seq
    stok = nseq * ntok
    scale = math.log2(math.e) / math.sqrt(nope + rope)
    assert rope <= LANES and nope == LANES and dk == LANES and dv == LANES

    x = (x_prompt.reshape(ptok, d), x_sample.reshape(stok, d))

    half = rope // 2
    inv = ROPE_THETA ** (-jnp.arange(half, dtype=F32) / half)
    pos = jnp.concatenate([jnp.tile(jnp.arange(seq, dtype=F32), bsz),
                           jnp.tile(jnp.arange(ntok, dtype=F32) + past_len, nseq)])
    ang = pos[:, None] * inv[None, :]
    cos = jnp.cos(ang)
    sin = jnp.sin(ang)
    cs = jnp.concatenate([_pad_to_lanes(jnp.concatenate([cos, cos], axis=1), 1),
                          _pad_to_lanes(jnp.concatenate([sin, sin], axis=1), 1)], axis=1)

    sp = [0]
    for w in (q_lora, kv_lora, rope, hd, hd, hb * dv, hb * dv, d, d):
        sp.append(sp[-1] + w)

    ckv_p, kpe_p, st_p, ckv_s, kpe_s, st_s = [], [], [], [], [], []
    for i in range(depth):
        wit = jnp.swapaxes(w_in[i], 0, 1)
        seg = [wit[sp[j]:sp[j + 1]] for j in range(9)]
        kr_t = seg[2]
        kr_rot_t = jnp.swapaxes(_rot_last(jnp.swapaxes(kr_t, 0, 1)), 0, 1)
        w_mlat = jnp.concatenate([seg[0], seg[1], _pad_to_lanes(kr_t, 0), _pad_to_lanes(kr_rot_t, 0)],
                                 axis=0).astype(BF16)
        w_hgt = jnp.concatenate([seg[3], seg[5], seg[6], seg[4]], axis=0).astype(BF16)
        w_gtt = jnp.concatenate([seg[7], seg[8]], axis=0).astype(BF16)
        wq = w_uq[i].reshape(q_lora, heads, nope + rope)
        wq_rope = wq[:, :, nope:]
        w_uq3 = jnp.concatenate([wq[:, :, :nope].reshape(q_lora, heads * nope),
                                 _pad_to_lanes(wq_rope, 2).reshape(q_lora, heads * LANES),
                                 _pad_to_lanes(_rot_last(wq_rope), 2).reshape(q_lora, heads * LANES)],
                                axis=1).astype(BF16)
        w_ukt = jnp.transpose(w_uk[i], (1, 2, 0)).astype(BF16)
        w_uv2 = w_uv[i].reshape(kv_lora, heads * vdim).astype(BF16)

        x, hmix = _ffn(x, ffn1_norm[i][None], ffn1_w_gate[i].astype(BF16), ffn1_w_up[i].astype(BF16),
                       ffn1_w_down[i].astype(BF16), mix_norm[i][None])

        sg = _gates(hmix, w_gtt)
        hv, lf = _hgrn_proj(hmix, w_hgt, hgrn_lb_logits, i)
        qc, kc, ckv_fp, ckv_fs, kpe_fp, kpe_fs = _mla_prep(
            hmix, w_mlat, q_norm[i][None], kv_norm[i][None], w_uq3, w_ukt, cs, ptok,
            heads=heads, q_lora=q_lora, kv_lora=kv_lora, nope=nope, rope=rope, scale=scale)

        npb = ptok // LANES
        oa_p = _attn_prompt(qc, kc, w_uv2, bsz, seq, heads=heads, vdim=vdim, c=kv_lora)
        cw = qc.shape[-1]
        qs = jnp.transpose(qc[npb:], (0, 2, 1, 3)).reshape(nseq, ntok * heads, cw)
        kn = jnp.pad(kc[ptok:].reshape(nseq, ntok, cw), ((0, 0), (0, 16 - ntok), (0, 0)))
        oa_s = _attn_sample(page_table, qs, kn, w_uv2, cache_ckv[i], jnp.swapaxes(cache_kpe[i], 1, 2),
                            heads=heads, vdim=vdim, ntok=ntok, c=kv_lora)
        oa = (oa_p, oa_s.reshape(stok, heads * vdim).astype(BF16))

        gn = hgrn_out_norm[i][None]
        ob_p, s_p = _hgrn(hv, lf, jnp.zeros((bsz, hb, dk, dv), F32), gn, seq // HGRN_CHUNK,
                          L=HGRN_CHUNK, G=1, sub=HGRN_SUB, n_t=HGRN_SUB, heads=hb, dk=dk, dv=dv)
        srows = HGRN_SAMPLE_ROWS
        pad = srows - ntok
        hv_s = jnp.pad(hv[ptok:].reshape(nseq, ntok, 4 * hd), ((0, 0), (0, pad), (0, 0)))
        lf_s = jnp.pad(lf[ptok:].reshape(nseq, ntok, hd), ((0, 0), (0, pad), (0, 0)))
        ob_s, s_s = _hgrn(hv_s.reshape(nseq * srows, 4 * hd), lf_s.reshape(nseq * srows, hd),
                          state_hgrn[i], gn, 1, L=srows, G=HGRN_SAMPLE_GROUP, sub=srows, n_t=ntok,
                          heads=hb, dk=dk, dv=dv)
        ob = (ob_p, ob_s.reshape(nseq, srows, hb * dv)[:, :ntok].reshape(stok, hb * dv))

        x = _merge(oa, ob, sg, x, w_branch_a[i].astype(BF16), w_branch_b[i].astype(BF16),
                   w_out[i].astype(BF16))
        x = _ffn(x, ffn2_norm[i][None], ffn2_w_gate[i].astype(BF16), ffn2_w_up[i].astype(BF16),
                 ffn2_w_down[i].astype(BF16))
        x = tuple(_ple(x, p_prompt[i].reshape(ptok, -1), p_sample[i].reshape(stok, -1), ple_norm[i][None],
                       w_ple_gate[i].astype(BF16), w_ple_proj[i].astype(BF16), final_norm[None],
                       i == depth - 1))

        ckv_p.append(ckv_fp.reshape(bsz, seq, kv_lora))
        kpe_p.append(kpe_fp.reshape(bsz, seq, rope))
        st_p.append(s_p)
        ckv_s.append(ckv_fs.reshape(nseq, ntok, kv_lora))
        kpe_s.append(kpe_fs.reshape(nseq, ntok, rope))
        st_s.append(s_s)

    return (x[0].reshape(bsz, seq, d), x[1].reshape(nseq, ntok, d),
            jnp.stack(ckv_p), jnp.stack(kpe_p), jnp.stack(st_p),
            jnp.stack(ckv_s), jnp.stack(kpe_s), jnp.stack(st_s))
```

```python
import functools
import math

import jax
import jax.numpy as jnp
from jax import lax
from jax.experimental import pallas as pl
from jax.experimental.pallas import tpu as pltpu

F32 = jnp.float32
BF16 = jnp.bfloat16

EPS = 1e-6
ROPE_THETA = 10000.0
PAGE_SIZE = 128
LANES = 128
HGRN_SUB = 16
HGRN_CHUNK = 64
HGRN_SAMPLE_ROWS = 16
HGRN_SAMPLE_GROUP = 4
ATTN_TQ = 128
ATTN_TK = 512
ATTN_ROW_CHUNKS = 4
SAMPLE_PAGES = 8
SAMPLE_BUFS = 4
FFN_MAX_ROWS = 1152
VMEM_LIMIT = 60 * 1024 * 1024


def _sigmoid(x):
    return 1.0 / (1.0 + jnp.exp(-x))


def _rms(x, g):
    return x * lax.rsqrt(jnp.mean(x * x, axis=-1, keepdims=True) + EPS) * g


def _dot(a, b):
    return jnp.dot(a, b, preferred_element_type=F32)


def _dot_nt(a, b):
    return lax.dot_general(a, b, (((1,), (1,)), ((), ())), preferred_element_type=F32)


def _dot_tn(a, b):
    return lax.dot_general(a, b, (((0,), (0,)), ((), ())), preferred_element_type=F32)


def _div_pow2(x, n):
    shift = int(math.log2(n))
    assert 2 ** shift == n
    return x >> shift


def _pick_tile(sizes, candidates):
    for c in candidates:
        if all(n % c == 0 for n in sizes):
            return c
    raise ValueError(f"no tile in {candidates} divides {sizes}")


def _resident(shape):
    nd = len(shape)
    return pl.BlockSpec(shape, lambda *_: (0,) * nd, pipeline_mode=pl.Buffered(1))


def _split_specs(tm, width, npb):
    return [pl.BlockSpec((tm, width), lambda i: (jnp.minimum(i, npb - 1), 0)),
            pl.BlockSpec((tm, width), lambda i: (jnp.maximum(i - npb, 0), 0))]


def _ffn_kernel(*refs, nf, post):
    refs = list(refs)
    x_ref, g_ref, wg_ref, wu_ref, wd_ref = refs[:5]
    refs = refs[5:]
    g2_ref = refs.pop(0) if post else None
    o_ref = refs.pop(0)
    h2_ref = refs.pop(0) if post else None
    h_scr, acc_scr = refs
    f = pl.program_id(1)

    @pl.when(f == 0)
    def _():
        h_scr[...] = _rms(x_ref[...], g_ref[...]).astype(BF16)
        acc_scr[...] = jnp.zeros(acc_scr.shape, F32)

    h = h_scr[...]
    g = _dot(h, wg_ref[...])
    u = _dot(h, wu_ref[...])
    a = (g * _sigmoid(g) * u).astype(BF16)
    acc_scr[...] += _dot(a, wd_ref[...])

    @pl.when(f == nf - 1)
    def _():
        y = x_ref[...] + 0.5 * acc_scr[...]
        o_ref[...] = y
        if post:
            h2_ref[...] = _rms(y, g2_ref[...]).astype(BF16)


def _ffn(x, g, wg, wu, wd, g2=None):
    n, d = x.shape
    dff = wg.shape[1]
    tf = _pick_tile([dff], (512, 256, 128))
    nf = dff // tf
    post = g2 is not None
    tm = next(t for t in range(min(FFN_MAX_ROWS, n), 0, -16) if n % t == 0)
    one = dict(pipeline_mode=pl.Buffered(1))
    in_specs = [
        pl.BlockSpec((tm, d), lambda i, f: (i, 0), **one),
        pl.BlockSpec((1, d), lambda i, f: (0, 0)),
        pl.BlockSpec((d, tf), lambda i, f: (0, f)),
        pl.BlockSpec((d, tf), lambda i, f: (0, f)),
        pl.BlockSpec((tf, d), lambda i, f: (f, 0)),
    ]
    args = [x, g, wg, wu, wd]
    out_shape = [jax.ShapeDtypeStruct((n, d), F32)]
    out_specs = [pl.BlockSpec((tm, d), lambda i, f: (i, 0), **one)]
    if post:
        in_specs.append(pl.BlockSpec((1, d), lambda i, f: (0, 0)))
        args.append(g2)
        out_shape.append(jax.ShapeDtypeStruct((n, d), BF16))
        out_specs.append(pl.BlockSpec((tm, d), lambda i, f: (i, 0), **one))
    res = pl.pallas_call(
        functools.partial(_ffn_kernel, nf=nf, post=post),
        grid=(n // tm, nf),
        in_specs=in_specs,
        out_specs=out_specs,
        out_shape=out_shape,
        scratch_shapes=[pltpu.VMEM((tm, d), BF16), pltpu.VMEM((tm, d), F32)],
        compiler_params=pltpu.CompilerParams(
            dimension_semantics=("parallel", "arbitrary"), vmem_limit_bytes=VMEM_LIMIT),
        name="ffn_post" if post else "ffn",
    )(*args)
    return res if post else res[0]


def _pair_tile(pair, candidates=(512, 256, 128)):
    ptok, stok = pair[0].shape[0], pair[1].shape[0]
    tm = _pick_tile([ptok, stok], candidates)
    return tm, ptok // tm, (ptok + stok) // tm


def _pair_rows(npb, p_ref, s_ref):
    return jnp.where(pl.program_id(0) < npb, p_ref[...], s_ref[...])


def _gates_kernel(hp_ref, hs_ref, w_ref, o_ref, *, tn, npb):
    h = _pair_rows(npb, hp_ref, hs_ref)
    for j in range(w_ref.shape[0] // tn):
        cols = slice(j * tn, (j + 1) * tn)
        o_ref[:, cols] = _sigmoid(_dot_nt(h, w_ref[cols, :])).astype(BF16)


def _gates(h, wt):
    d = h[0].shape[1]
    nc = wt.shape[0]
    tm, npb, nblk = _pair_tile(h)
    tn = _pick_tile([nc], (1024, 512, 256, 128))
    return pl.pallas_call(
        functools.partial(_gates_kernel, tn=tn, npb=npb),
        grid=(nblk,),
        in_specs=_split_specs(tm, d, npb) + [_resident(wt.shape)],
        out_specs=pl.BlockSpec((tm, nc), lambda i: (i, 0)),
        out_shape=jax.ShapeDtypeStruct((nblk * tm, nc), BF16),
        compiler_params=pltpu.CompilerParams(
            dimension_semantics=("parallel",), vmem_limit_bytes=VMEM_LIMIT),
        name="gates",
    )(h[0], h[1], wt)


def _hgrn_proj_kernel(hp_ref, hs_ref, w_ref, lbl_ref, hv_ref, lf_ref, *, layer, c, npb):
    h = _pair_rows(npb, hp_ref, hs_ref)

    def proj(j):
        return _dot_nt(h, w_ref[j * c:(j + 1) * c, :])

    for j in range(2):
        hv_ref[:, j * c:(j + 1) * c] = proj(j).astype(BF16)
    y = proj(2)
    hv_ref[:, 2 * c:3 * c] = (y * _sigmoid(y)).astype(BF16)
    y = proj(3)
    logits = lbl_ref[...]
    e = jnp.exp(logits - jnp.max(logits, axis=0, keepdims=True))
    lb = jnp.sum(e[:layer + 1], axis=0, keepdims=True) / jnp.sum(e, axis=0, keepdims=True)
    lf_ref[...] = jnp.log(lb + (1.0 - lb) * _sigmoid(y))
    hv_ref[:, 3 * c:4 * c] = ((1.0 - lb) * _sigmoid(-y)).astype(BF16)


def _hgrn_proj(h, wt, lb_logits, layer):
    d = h[0].shape[1]
    c = wt.shape[0] // 4
    tm, npb, nblk = _pair_tile(h)
    n = nblk * tm
    return pl.pallas_call(
        functools.partial(_hgrn_proj_kernel, layer=layer, c=c, npb=npb),
        grid=(nblk,),
        in_specs=_split_specs(tm, d, npb) + [_resident(wt.shape), _resident(lb_logits.shape)],
        out_specs=[pl.BlockSpec((tm, 4 * c), lambda i: (i, 0)),
                   pl.BlockSpec((tm, c), lambda i: (i, 0))],
        out_shape=[jax.ShapeDtypeStruct((n, 4 * c), BF16),
                   jax.ShapeDtypeStruct((n, c), F32)],
        compiler_params=pltpu.CompilerParams(
            dimension_semantics=("parallel",), vmem_limit_bytes=VMEM_LIMIT),
        name="hgrn_proj",
    )(h[0], h[1], wt, lb_logits)


def _mla_prep_kernel(hp_ref, hs_ref, wm_ref, qn_ref, kvn_ref, wuq_ref, wuk_ref, cs_ref,
                     qc_ref, kc_ref, ckvp_ref, ckvs_ref, kpep_ref, kpes_ref,
                     *, heads, q_lora, kv_lora, nope, rope, scale, npb):
    i = pl.program_id(0)
    tm = hp_ref.shape[0]
    c = _dot_nt(_pair_rows(npb, hp_ref, hs_ref), wm_ref[...])
    cos = cs_ref[:, :LANES]
    sin = cs_ref[:, LANES:]
    ckv = _rms(c[:, q_lora:q_lora + kv_lora], kvn_ref[...])
    o = q_lora + kv_lora
    kpe = c[:, o:o + LANES] * cos + c[:, o + LANES:o + 2 * LANES] * sin
    kc_ref[:, :kv_lora] = ckv.astype(BF16)
    kc_ref[:, kv_lora:] = kpe.astype(BF16)

    @pl.when(i < npb)
    def _():
        ckvp_ref[...] = ckv
        kpep_ref[...] = kpe[:, :rope]

    @pl.when(i >= npb)
    def _():
        ckvs_ref[...] = ckv
        kpes_ref[...] = kpe[:, :rope]

    qn = _rms(c[:, :q_lora], qn_ref[...]).astype(BF16)
    q3 = _dot(qn, wuq_ref[...]) * scale
    nb = tm // LANES
    for h in range(heads):
        qh = q3[:, h * nope:(h + 1) * nope].astype(BF16)
        lat = _dot(qh, wuk_ref[h]).astype(BF16)
        qc_ref[:, h, :, :kv_lora] = lat.reshape(nb, LANES, kv_lora)
        r0 = heads * nope + h * LANES
        r1 = heads * nope + heads * LANES + h * LANES
        pe = (q3[:, r0:r0 + LANES] * cos + q3[:, r1:r1 + LANES] * sin).astype(BF16)
        qc_ref[:, h, :, kv_lora:] = pe.reshape(nb, LANES, LANES)


def _mla_prep(h, w_mlat, q_norm, kv_norm, w_uq3, w_ukt, cs, *, heads, q_lora, kv_lora, nope, rope, scale):
    d = h[0].shape[1]
    ptok, stok = h[0].shape[0], h[1].shape[0]
    tm, npb, nblk = _pair_tile(h)
    n = nblk * tm
    nb = tm // LANES
    cw = kv_lora + LANES
    kern = functools.partial(_mla_prep_kernel, heads=heads, q_lora=q_lora, kv_lora=kv_lora,
                             nope=nope, rope=rope, scale=scale, npb=npb)
    return pl.pallas_call(
        kern,
        grid=(nblk,),
        in_specs=_split_specs(tm, d, npb)
                 + [_resident(w_mlat.shape), _resident(q_norm.shape), _resident(kv_norm.shape),
                    _resident(w_uq3.shape), _resident(w_ukt.shape),
                    pl.BlockSpec((tm, 2 * LANES), lambda i: (i, 0))],
        out_specs=[pl.BlockSpec((nb, heads, LANES, cw), lambda i: (i, 0, 0, 0)),
                   pl.BlockSpec((tm, cw), lambda i: (i, 0))]
                  + _split_specs(tm, kv_lora, npb) + _split_specs(tm, rope, npb),
        out_shape=[jax.ShapeDtypeStruct((n // LANES, heads, LANES, cw), BF16),
                   jax.ShapeDtypeStruct((n, cw), BF16),
                   jax.ShapeDtypeStruct((ptok, kv_lora), F32),
                   jax.ShapeDtypeStruct((stok, kv_lora), F32),
                   jax.ShapeDtypeStruct((ptok, rope), F32),
                   jax.ShapeDtypeStruct((stok, rope), F32)],
        compiler_params=pltpu.CompilerParams(
            dimension_semantics=("arbitrary",), vmem_limit_bytes=VMEM_LIMIT),
        name="mla_prep",
    )(h[0], h[1], w_mlat, q_norm, kv_norm, w_uq3, w_ukt, cs)


def _attn_prompt_kernel(q_ref, k_ref, wuv_ref, o_ref, m_scr, l_scr, acc_scr,
                        *, tq, tk, heads, vdim, c, nchunk):
    i = pl.program_id(1)
    hpc = heads // nchunk
    rc = hpc * tq
    m_scr[...] = jnp.full(m_scr.shape, -jnp.inf, F32)
    l_scr[...] = jnp.zeros(l_scr.shape, F32)
    acc_scr[...] = jnp.zeros(acc_scr.shape, F32)

    nt = tk // LANES

    def step(j, masked):
        off = pl.multiple_of(j * tk, tk)
        k = k_ref[pl.ds(off, tk), :]
        v = k[:, :c]
        if masked:
            tok = i * tq + (lax.broadcasted_iota(jnp.int32, (rc, 1), 0) & (tq - 1))
            col = off + lax.broadcasted_iota(jnp.int32, (1, tk), 1)
            keep = col <= tok

        def scores(r):
            q = q_ref[0, r * hpc:(r + 1) * hpc].reshape(rc, q_ref.shape[-1])
            s = _dot_nt(q, k)
            return jnp.where(keep, s, -jnp.inf) if masked else s

        def update(r, s):
            rs = slice(r * rc, (r + 1) * rc)
            m_prev = m_scr[rs]
            smax = s[:, :LANES]
            for t in range(1, nt):
                smax = jnp.maximum(smax, s[:, t * LANES:(t + 1) * LANES])
            m_new = jnp.maximum(m_prev, jnp.max(smax, axis=-1, keepdims=True))
            alpha = jnp.exp2(m_prev - m_new)
            p = jnp.exp2(s - jnp.concatenate([m_new] * nt, axis=1))
            psum = p[:, :LANES]
            for t in range(1, nt):
                psum = psum + p[:, t * LANES:(t + 1) * LANES]
            l_scr[rs] = alpha * l_scr[rs] + psum
            acc_scr[rs] = jnp.concatenate([alpha] * (c // LANES), axis=1) * acc_scr[rs] + _dot(p.astype(BF16), v)
            m_scr[rs] = m_new

        s_cur = scores(0)
        for r in range(nchunk):
            s_next = scores(r + 1) if r + 1 < nchunk else None
            update(r, s_cur)
            s_cur = s_next

    nfull = (i * tq) // tk

    def body(j, carry):
        step(j, False)
        return carry

    lax.fori_loop(0, nfull, body, 0)
    step(nfull, True)

    inv = 1.0 / jnp.sum(l_scr[...], axis=-1, keepdims=True)
    for h in range(heads):
        oh = (acc_scr[h * tq:(h + 1) * tq, :] * inv[h * tq:(h + 1) * tq]).astype(BF16)
        o_ref[:, h * vdim:(h + 1) * vdim] = _dot(oh, wuv_ref[:, h * vdim:(h + 1) * vdim]).astype(BF16)


def _attn_prompt(qc, kc, wuv, bsz, seq, *, heads, vdim, c):
    cw = kc.shape[1]
    tq = ATTN_TQ
    tk = min(ATTN_TK, seq)
    nq = seq // tq
    rows = heads * tq
    nchunk = ATTN_ROW_CHUNKS
    kern = functools.partial(_attn_prompt_kernel, tq=tq, tk=tk, heads=heads, vdim=vdim, c=c, nchunk=nchunk)
    return pl.pallas_call(
        kern,
        grid=(bsz, nq),
        in_specs=[pl.BlockSpec((1, heads, tq, cw), lambda bi, i: (bi * nq + i, 0, 0, 0)),
                  pl.BlockSpec((seq, cw), lambda bi, i: (bi, 0)),
                  _resident(wuv.shape)],
        out_specs=pl.BlockSpec((tq, heads * vdim), lambda bi, i: (bi * nq + i, 0)),
        out_shape=jax.ShapeDtypeStruct((bsz * seq, heads * vdim), BF16),
        scratch_shapes=[pltpu.VMEM((rows, LANES), F32), pltpu.VMEM((rows, LANES), F32),
                        pltpu.VMEM((rows, c), F32)],
        compiler_params=pltpu.CompilerParams(
            dimension_semantics=("parallel", "arbitrary"), vmem_limit_bytes=VMEM_LIMIT),
        name="attn_prompt",
    )(qc, kc, wuv)


def _attn_sample_kernel(pt_ref, q_ref, kn_ref, wuv_ref, cck_ref, cpe_ref, o_ref,
                        bufk, bufp, sem, m_scr, l_scr, acc_scr,
                        *, nseq, npages, ch, nbuf, heads, vdim, ntok, c, rope):
    cps = npages // ch
    total = (nseq // 2) * cps

    def unit_copies(unit, slot):
        pair = unit // cps
        part = unit % cps
        copies = []
        for sq in range(2):
            base = (2 * pair + sq) * npages + part * ch
            for pg in range(ch):
                page = pt_ref[base + pg]
                copies.append(pltpu.make_async_copy(
                    cck_ref.at[page], bufk.at[slot, sq, pl.ds(pg * PAGE_SIZE, PAGE_SIZE), :],
                    sem.at[0, slot]))
                copies.append(pltpu.make_async_copy(
                    cpe_ref.at[page], bufp.at[slot, sq, :, pl.ds(pg * PAGE_SIZE, PAGE_SIZE)],
                    sem.at[1, slot]))
        return copies

    def start_unit(unit):
        for cp in unit_copies(unit, unit % nbuf):
            cp.start()

    def wait_unit(unit):
        for cp in unit_copies(unit, unit % nbuf):
            cp.wait()

    for u0 in range(min(nbuf - 1, total)):
        start_unit(u0)

    def softmax_update(sq, s, v):
        m_prev = m_scr[sq]
        m_new = jnp.maximum(m_prev, jnp.max(s, axis=-1, keepdims=True))
        alpha = jnp.exp2(m_prev - m_new)
        p = jnp.exp2(s - m_new)
        l_scr[sq] = alpha * l_scr[sq] + jnp.sum(p, axis=-1, keepdims=True)
        acc_scr[sq] = alpha * acc_scr[sq] + _dot(p.astype(BF16), v)
        m_scr[sq] = m_new

    def body(u, carry):
        pair = u // cps
        part = u % cps
        slot = u % nbuf

        @pl.when(u + nbuf - 1 < total)
        def _():
            start_unit(u + nbuf - 1)

        wait_unit(u)

        @pl.when(part == 0)
        def _():
            m_scr[...] = jnp.full(m_scr.shape, -jnp.inf, F32)
            l_scr[...] = jnp.zeros(l_scr.shape, F32)
            acc_scr[...] = jnp.zeros(acc_scr.shape, F32)

        kbs, ss = [], []
        for sq in range(2):
            q = q_ref[2 * pair + sq]
            kb = bufk[slot, sq].astype(BF16)
            kpt = bufp[slot, sq].astype(BF16)
            kbs.append(kb)
            ss.append(_dot_nt(q[:, :c], kb) + _dot(q[:, c:c + rope], kpt))
        for sq in range(2):
            softmax_update(sq, ss[sq], kbs[sq])

        @pl.when(part == cps - 1)
        def _():
            for sq in range(2):
                seq = 2 * pair + sq
                q = q_ref[seq]
                kn = kn_ref[seq]
                sn = _dot_nt(q, kn)
                rows, cols = sn.shape
                t_row = _div_pow2(lax.broadcasted_iota(jnp.int32, (rows, 1), 0), heads)
                col = lax.broadcasted_iota(jnp.int32, (1, cols), 1)
                sn = jnp.where((col <= t_row) & (col < ntok), sn, -jnp.inf)
                softmax_update(sq, sn, kn[:, :c])
                o = (acc_scr[sq] / l_scr[sq]).astype(BF16)
                y = _dot(o, wuv_ref[...])
                h_row = lax.broadcasted_iota(jnp.int32, (rows, 1), 0) & (heads - 1)
                h_col = _div_pow2(lax.broadcasted_iota(jnp.int32, (1, heads * vdim), 1), vdim)
                z = jnp.where(h_row == h_col, y, 0.0)
                tid = lax.broadcasted_iota(jnp.int32, (ntok, 1), 0)
                out = jnp.zeros((ntok, heads * vdim), F32)
                for t in range(ntok):
                    zt = jnp.sum(z[t * heads:(t + 1) * heads], axis=0, keepdims=True)
                    out = jnp.where(tid == t, zt, out)
                o_ref[seq] = out

        return carry

    lax.fori_loop(0, total, body, 0)


def _attn_sample(page_table, qs, kn, wuv, cache_ckv, cache_kpet, *, heads, vdim, ntok, c):
    nseq, rows, _ = qs.shape
    npages = page_table.shape[1]
    rope = cache_kpet.shape[1]
    ch = min(SAMPLE_PAGES, npages)
    assert npages % ch == 0 and nseq % 2 == 0
    nbuf = SAMPLE_BUFS
    kern = functools.partial(_attn_sample_kernel, nseq=nseq, npages=npages, ch=ch, nbuf=nbuf,
                             heads=heads, vdim=vdim, ntok=ntok, c=c, rope=rope)
    vmem = pl.BlockSpec(memory_space=pltpu.VMEM)
    return pl.pallas_call(
        kern,
        in_specs=[pl.BlockSpec(memory_space=pltpu.SMEM), vmem, vmem, vmem,
                  pl.BlockSpec(memory_space=pl.ANY), pl.BlockSpec(memory_space=pl.ANY)],
        out_specs=vmem,
        out_shape=jax.ShapeDtypeStruct((nseq, ntok, heads * vdim), F32),
        scratch_shapes=[pltpu.VMEM((nbuf, 2, ch * PAGE_SIZE, c), F32),
                        pltpu.VMEM((nbuf, 2, rope, ch * PAGE_SIZE), F32),
                        pltpu.SemaphoreType.DMA((2, nbuf)),
                        pltpu.VMEM((2, rows, 1), F32), pltpu.VMEM((2, rows, 1), F32),
                        pltpu.VMEM((2, rows, c), F32)],
        compiler_params=pltpu.CompilerParams(vmem_limit_bytes=VMEM_LIMIT),
        name="attn_sample",
    )(page_table.reshape(-1), qs, kn, wuv, cache_ckv, cache_kpet)


def _hgrn_kernel(hv_ref, lf_ref, s0_ref, gn_ref, ob_ref, so_ref,
                 s_scr, q_scr, k_scr, b_scr, at_scr,
                 *, L, G, sub, n_t, heads, dk, dv):
    c = pl.program_id(1)
    nc = pl.num_programs(1)

    @pl.when(c == 0)
    def _():
        s_scr[...] = s0_ref[...]

    for gi in range(G):
        _hgrn_chunk(hv_ref, lf_ref, gn_ref, ob_ref, s_scr, q_scr, k_scr, b_scr, at_scr,
                    gi, L=L, sub=sub, n_t=n_t, heads=heads, dk=dk, dv=dv)

    @pl.when(c == nc - 1)
    def _():
        so_ref[...] = s_scr[...]


def _hgrn_chunk(hv_ref, lf_ref, gn_ref, ob_ref, s_scr, q_scr, k_scr, b_scr, at_scr,
                gi, *, L, sub, n_t, heads, dk, dv):
    hd = heads * dk
    nsub = L // sub
    levels = int(math.log2(nsub))
    assert 2 ** levels == nsub
    rows = slice(gi * L, (gi + 1) * L)

    g = lf_ref[rows, :]
    g1 = g.astype(BF16)
    r1 = g - g1.astype(F32)
    g2 = r1.astype(BF16)
    g3 = (r1 - g2.astype(F32)).astype(BF16)
    row = lax.broadcasted_iota(jnp.int32, (L, L), 0)
    col = lax.broadcasted_iota(jnp.int32, (L, L), 1)
    tril = jnp.where(col <= row, 1.0, 0.0).astype(BF16)
    b = _dot(tril, g1) + _dot(tril, g2) + _dot(tril, g3)

    def group_bounds(grp):
        ends = [b[(j + 1) * grp - 1:(j + 1) * grp, :] for j in range(L // grp)]
        starts = [jnp.zeros_like(ends[0])] + ends[:-1]
        rep = lambda rows_: jnp.concatenate([jnp.broadcast_to(r, (grp, hd)) for r in rows_], axis=0)
        return rep(starts), rep(ends)

    bstart, bend = group_bounds(sub)
    brel = b - bstart
    blast = b[L - 1:L, :]

    q = hv_ref[rows, 0:hd].astype(F32)
    v = hv_ref[rows, hd:2 * hd]
    sg = hv_ref[rows, 2 * hd:3 * hd].astype(F32)
    k = hv_ref[rows, 3 * hd:4 * hd].astype(F32)

    q_scr[rows, :] = q
    k_scr[rows, :] = k
    b_scr[rows, :] = brel * math.log2(math.e)

    qt = q * jnp.exp(brel)
    kh = k * jnp.exp(bend - b)
    q_in = (qt * jnp.exp(bstart)).astype(BF16)
    k_st = (kh * jnp.exp(blast - bend)).astype(BF16)
    e_last = jnp.exp(blast)

    lhs = [qt.astype(BF16)]
    rhs = [kh.astype(BF16)]
    valid = []
    for lv in range(levels):
        grp = sub * (2 ** lv)
        if lv > 0:
            gs, ge = group_bounds(grp)
            lhs.append((qt * jnp.exp(bstart - gs)).astype(BF16))
            rhs.append((kh * jnp.exp(ge - bend)).astype(BF16))
        cg = _div_pow2(col, grp)
        valid.append(((cg & 1) == 1) & (_div_pow2(row, grp) == cg - 1))

    rid8 = lax.broadcasted_iota(jnp.int32, (8, 1), 0)
    lane = lax.broadcasted_iota(jnp.int32, (8, dk), 1)
    lss = [slice(h * dk, (h + 1) * dk) for h in range(heads)]
    halves = sub // 8

    def diag_body(i, carry):
        r0 = pl.multiple_of(gi * L + i * sub, sub)
        at = [[jnp.zeros((8, dk), F32) for _ in range(halves)] for _ in range(heads)]
        blk = [[ref[pl.ds(r0, sub), ls] for ref in (q_scr, k_scr, b_scr)] for ls in lss]
        for t in range(n_t):
            is_col = lane == i * sub + t
            for h in range(heads):
                qb, kb, bb = blk[h]
                for hf in range(t // 8 + 1):
                    rr = slice(8 * hf, 8 * hf + 8)
                    w = jnp.exp2(bb[t:t + 1, :] - bb[rr]) * kb[rr] * qb[t:t + 1, :]
                    a = jnp.sum(w, axis=-1, keepdims=True)
                    if t < 8 * hf + 7:
                        a = jnp.where(rid8 + 8 * hf <= t, a, 0.0)
                    at[h][hf] = jnp.where(is_col, a, at[h][hf])
        for h, ls in enumerate(lss):
            for hf in range(halves):
                at_scr[pl.ds(pl.multiple_of(r0 + 8 * hf, 8), 8), ls] = at[h][hf]
        return carry

    lax.fori_loop(0, nsub, diag_body, 0)

    gn = gn_ref[...]
    s_prev = [s_scr[gi, h] for h in range(heads)]
    o_in = [_dot(q_in[:, ls], s_prev[h].astype(BF16)) for h, ls in enumerate(lss)]
    p_off = [[_dot_nt(rhs[lv][:, ls], lhs[lv][:, ls]) for lv in range(levels)] for ls in lss]
    s_upd = [_dot_tn(k_st[:, ls], v[:, ls]) for ls in lss]
    o_mix = []
    for h, ls in enumerate(lss):
        if levels > 0:
            a_t = at_scr[rows, h * dk:h * dk + L]
            for lv in range(levels):
                a_t = a_t + jnp.where(valid[lv], p_off[h][lv], 0.0)
            o_mix.append(_dot_tn(a_t.astype(BF16), v[:, ls]))
        else:
            o_mix.append(_dot_tn(at_scr[rows, ls].astype(BF16), v[:, ls])[:L])
    for h, ls in enumerate(lss):
        decay = jnp.transpose(jnp.broadcast_to(e_last[:, ls], (dk, dk)))
        s_scr[gi, h] = decay * s_prev[h] + s_upd[h]
        o = o_in[h] + o_mix[h]
        on = o * lax.rsqrt(jnp.mean(o * o, axis=-1, keepdims=True) + EPS) * gn
        ob_ref[rows, ls] = (on * sg[:, ls]).astype(BF16)


def _hgrn(hv, lf, s0, gn, nc, *, L, G, sub, n_t, heads, dk, dv):
    nseq = s0.shape[0]
    rows = nseq * nc * L
    hd = heads * dk
    assert nseq % G == 0 and (G == 1 or nc == 1)
    kern = functools.partial(_hgrn_kernel, L=L, G=G, sub=sub, n_t=n_t, heads=heads, dk=dk, dv=dv)
    return pl.pallas_call(
        kern,
        grid=(nseq // G, nc),
        in_specs=[pl.BlockSpec((G * L, 4 * hd), lambda s, c: (s * nc + c, 0)),
                  pl.BlockSpec((G * L, hd), lambda s, c: (s * nc + c, 0)),
                  pl.BlockSpec((G, heads, dk, dv), lambda s, c: (s, 0, 0, 0)),
                  pl.BlockSpec((1, dv), lambda s, c: (0, 0))],
        out_specs=[pl.BlockSpec((G * L, heads * dv), lambda s, c: (s * nc + c, 0)),
                   pl.BlockSpec((G, heads, dk, dv), lambda s, c: (s, 0, 0, 0))],
        out_shape=[jax.ShapeDtypeStruct((rows, heads * dv), BF16),
                   jax.ShapeDtypeStruct((nseq, heads, dk, dv), F32)],
        scratch_shapes=[pltpu.VMEM((G, heads, dk, dv), F32)] + [pltpu.VMEM((G * L, hd), F32)] * 4,
        compiler_params=pltpu.CompilerParams(
            dimension_semantics=("parallel", "arbitrary"), vmem_limit_bytes=VMEM_LIMIT),
        name="hgrn",
    )(hv, lf, s0, gn)


def _merge_kernel(oap_ref, oas_ref, obp_ref, obs_ref, xp_ref, xs_ref, sg_ref, wa_ref, wb_ref, wo_ref, o_ref,
                  *, d, npb):
    a = _dot(_pair_rows(npb, oap_ref, oas_ref), wa_ref[...])
    b = _dot(_pair_rows(npb, obp_ref, obs_ref), wb_ref[...])
    merged = sg_ref[:, :d].astype(F32) * a + sg_ref[:, d:].astype(F32) * b
    o_ref[...] = _pair_rows(npb, xp_ref, xs_ref) + _dot(merged.astype(BF16), wo_ref[...])


def _merge(oa, ob, x, sg, wa, wb, wo):
    d = x[0].shape[1]
    tm, npb, nblk = _pair_tile(x, (256, 128))
    return pl.pallas_call(
        functools.partial(_merge_kernel, d=d, npb=npb),
        grid=(nblk,),
        in_specs=_split_specs(tm, oa[0].shape[1], npb) + _split_specs(tm, ob[0].shape[1], npb)
                 + _split_specs(tm, d, npb)
                 + [pl.BlockSpec((tm, 2 * d), lambda i: (i, 0)),
                    _resident(wa.shape), _resident(wb.shape), _resident(wo.shape)],
        out_specs=pl.BlockSpec((tm, d), lambda i: (i, 0)),
        out_shape=jax.ShapeDtypeStruct((nblk * tm, d), F32),
        compiler_params=pltpu.CompilerParams(
            dimension_semantics=("parallel",), vmem_limit_bytes=VMEM_LIMIT),
        name="merge_out",
    )(oa[0], oa[1], ob[0], ob[1], x[0], x[1], sg, wa, wb, wo)


def _ple_kernel(x_ref, pp_ref, ps_ref, g_ref, wg_ref, wp_ref, gf_ref, yp_ref, ys_ref, *, final, npb):
    i = pl.program_id(0)
    x = x_ref[...]
    gate = _sigmoid(_dot(_rms(x, g_ref[...]).astype(BF16), wg_ref[...]))
    p = jnp.where(i < npb, pp_ref[...], ps_ref[...])
    y = x + gate * _dot(p.astype(BF16), wp_ref[...])
    if final:
        y = _rms(y, gf_ref[...])

    @pl.when(i < npb)
    def _():
        yp_ref[...] = y

    @pl.when(i >= npb)
    def _():
        ys_ref[...] = y


def _ple(x, pp, ps, g, wg, wp, gf, final):
    n, d = x.shape
    ptok, stok = pp.shape[0], ps.shape[0]
    tm = _pick_tile([ptok, stok], (512, 256, 128))
    npb = ptok // tm
    return pl.pallas_call(
        functools.partial(_ple_kernel, final=final, npb=npb),
        grid=(n // tm,),
        in_specs=[pl.BlockSpec((tm, d), lambda i: (i, 0))] + _split_specs(tm, pp.shape[1], npb)
                 + [_resident(g.shape), _resident(wg.shape), _resident(wp.shape), _resident(gf.shape)],
        out_specs=_split_specs(tm, d, npb),
        out_shape=[jax.ShapeDtypeStruct((ptok, d), F32), jax.ShapeDtypeStruct((stok, d), F32)],
        compiler_params=pltpu.CompilerParams(
            dimension_semantics=("arbitrary",), vmem_limit_bytes=VMEM_LIMIT),
        name="ple",
    )(x, pp, ps, g, wg, wp, gf)


def _rot_last(w):
    half = w.shape[-1] // 2
    return jnp.concatenate([-w[..., half:], w[..., :half]], axis=-1)


def _pad_to_lanes(w, axis):
    pads = [(0, 0)] * w.ndim
    pads[axis] = (0, LANES - w.shape[axis])
    return jnp.pad(w, pads)


def kernel(x_prompt, x_sample, cache_ckv, cache_kpe, state_hgrn, page_table, p_prompt, p_sample, ffn1_norm, ffn1_w_gate, ffn1_w_up, ffn1_w_down, mix_norm, w_in, q_norm, w_uq, kv_norm, w_uk, w_uv, hgrn_lb_logits, hgrn_out_norm, w_branch_a, w_branch_b, w_out, ffn2_norm, ffn2_w_gate, ffn2_w_up, ffn2_w_down, ple_norm, w_ple_gate, w_ple_proj, final_norm):
    bsz, seq, d = x_prompt.shape
    nseq, ntok, _ = x_sample.shape
    depth = w_in.shape[0]
    q_lora = q_norm.shape[1]
    kv_lora, heads, nope = w_uk.shape[1:]
    vdim = w_uv.shape[-1]
    rope = cache_kpe.shape[-1]
    hb, dk, dv = state_hgrn.shape[2:]
    hd = hb * dk
    npages = page_table.shape[1]
    past_len = npages * PAGE_SIZE
    ptok = bsz * seq
    stok = nseq * ntok
    scale = math.log2(math.e) / math.sqrt(nope + rope)
    assert rope <= LANES and nope == LANES and dk == LANES and dv == LANES

    x = (x_prompt.reshape(ptok, d), x_sample.reshape(stok, d))

    half = rope // 2
    inv = ROPE_THETA ** (-jnp.arange(half, dtype=F32) / half)
    pos = jnp.concatenate([jnp.tile(jnp.arange(seq, dtype=F32), bsz),
                           jnp.tile(jnp.arange(ntok, dtype=F32) + past_len, nseq)])
    ang = pos[:, None] * inv[None, :]
    cos = jnp.cos(ang)
    sin = jnp.sin(ang)
    cs = jnp.concatenate([_pad_to_lanes(jnp.concatenate([cos, cos], axis=1), 1),
                          _pad_to_lanes(jnp.concatenate([sin, sin], axis=1), 1)], axis=1)

    sp = [0]
    for w in (q_lora, kv_lora, rope, hd, hd, hb * dv, hb * dv, d, d):
        sp.append(sp[-1] + w)

    ckv_p, kpe_p, st_p, ckv_s, kpe_s, st_s = [], [], [], [], [], []
    for i in range(depth):
        wit = jnp.swapaxes(w_in[i], 0, 1)
        seg = [wit[sp[j]:sp[j + 1]] for j in range(9)]
        kr_t = seg[2]
        kr_rot_t = jnp.swapaxes(_rot_last(jnp.swapaxes(kr_t, 0, 1)), 0, 1)
        w_mlat = jnp.concatenate([seg[0], seg[1], _pad_to_lanes(kr_t, 0), _pad_to_lanes(kr_rot_t, 0)],
                                 axis=0).astype(BF16)
        w_hgt = jnp.concatenate([seg[3], seg[5], seg[6], seg[4]], axis=0).astype(BF16)
        w_gtt = jnp.concatenate([seg[7], seg[8]], axis=0).astype(BF16)
        wq = w_uq[i].reshape(q_lora, heads, nope + rope)
        wq_rope = wq[:, :, nope:]
        w_uq3 = jnp.concatenate([wq[:, :, :nope].reshape(q_lora, heads * nope),
                                 _pad_to_lanes(wq_rope, 2).reshape(q_lora, heads * LANES),
                                 _pad_to_lanes(_rot_last(wq_rope), 2).reshape(q_lora, heads * LANES)],
                                axis=1).astype(BF16)
        w_ukt = jnp.transpose(w_uk[i], (1, 2, 0)).astype(BF16)
        w_uv2 = w_uv[i].reshape(kv_lora, heads * vdim).astype(BF16)

        ffn1 = (ffn1_norm[i][None], ffn1_w_gate[i].astype(BF16), ffn1_w_up[i].astype(BF16),
                ffn1_w_down[i].astype(BF16), mix_norm[i][None])
        (x_p, hmix_p), (x_s, hmix_s) = _ffn(x[0], *ffn1), _ffn(x[1], *ffn1)
        x, hmix = (x_p, x_s), (hmix_p, hmix_s)

        sg = _gates(hmix, w_gtt)
        hv, lf = _hgrn_proj(hmix, w_hgt, hgrn_lb_logits, i)
        qc, kc, ckv_fp, ckv_fs, kpe_fp, kpe_fs = _mla_prep(
            hmix, w_mlat, q_norm[i][None], kv_norm[i][None], w_uq3, w_ukt, cs,
            heads=heads, q_lora=q_lora, kv_lora=kv_lora, nope=nope, rope=rope, scale=scale)

        npb = ptok // LANES
        oa_p = _attn_prompt(qc, kc, w_uv2, bsz, seq, heads=heads, vdim=vdim, c=kv_lora)
        cw = qc.shape[-1]
        qs = jnp.transpose(qc[npb:], (0, 2, 1, 3)).reshape(nseq, ntok * heads, cw)
        kn = jnp.pad(kc[ptok:].reshape(nseq, ntok, cw), ((0, 0), (0, 16 - ntok), (0, 0)))
        oa_s = _attn_sample(page_table, qs, kn, w_uv2, cache_ckv[i], jnp.swapaxes(cache_kpe[i], 1, 2),
                            heads=heads, vdim=vdim, ntok=ntok, c=kv_lora)
        oa = (oa_p, oa_s.reshape(stok, heads * vdim).astype(BF16))

        gn = hgrn_out_norm[i][None]
        ob_p, s_p = _hgrn(hv, lf, jnp.zeros((bsz, hb, dk, dv), F32), gn, seq // HGRN_CHUNK,
                          L=HGRN_CHUNK, G=1, sub=HGRN_SUB, n_t=HGRN_SUB, heads=hb, dk=dk, dv=dv)
        srows = HGRN_SAMPLE_ROWS
        pad = srows - ntok
        hv_s = jnp.pad(hv[ptok:].reshape(nseq, ntok, 4 * hd), ((0, 0), (0, pad), (0, 0)))
        lf_s = jnp.pad(lf[ptok:].reshape(nseq, ntok, hd), ((0, 0), (0, pad), (0, 0)))
        ob_s, s_s = _hgrn(hv_s.reshape(nseq * srows, 4 * hd), lf_s.reshape(nseq * srows, hd),
                          state_hgrn[i], gn, 1, L=srows, G=HGRN_SAMPLE_GROUP, sub=srows, n_t=ntok,
                          heads=hb, dk=dk, dv=dv)
        ob = (ob_p, ob_s.reshape(nseq, srows, hb * dv)[:, :ntok].reshape(stok, hb * dv))

        x = _merge(oa, ob, x, sg, w_branch_a[i].astype(BF16), w_branch_b[i].astype(BF16),
                   w_out[i].astype(BF16))
        x = _ffn(x, ffn2_norm[i][None], ffn2_w_gate[i].astype(BF16), ffn2_w_up[i].astype(BF16),
                 ffn2_w_down[i].astype(BF16))
        x = tuple(_ple(x, p_prompt[i].reshape(ptok, -1), p_sample[i].reshape(stok, -1), ple_norm[i][None],
                       w_ple_gate[i].astype(BF16), w_ple_proj[i].astype(BF16), final_norm[None],
                       i == depth - 1))

        ckv_p.append(ckv_fp.reshape(bsz, seq, kv_lora))
        kpe_p.append(kpe_fp.reshape(bsz, seq, rope))
        st_p.append(s_p)
        ckv_s.append(ckv_fs.reshape(nseq, ntok, kv_lora))
        kpe_s.append(kpe_fs.reshape(nseq, ntok, rope))
        st_s.append(s_s)

    return (x[0].reshape(bsz, seq, d), x[1].reshape(nseq, ntok, d),
            jnp.stack(ckv_p), jnp.stack(kpe_p), jnp.stack(st_p),
            jnp.stack(ckv_s), jnp.stack(kpe_s), jnp.stack(st_s))
```

```python
import functools
import math

import jax
import jax.numpy as jnp
from jax import lax
from jax.experimental import pallas as pl
from jax.experimental.pallas import tpu as pltpu

F32 = jnp.float32
BF16 = jnp.bfloat16

EPS = 1e-6
ROPE_THETA = 10000.0
PAGE_SIZE = 128
LANES = 128
HGRN_SUB = 16
HGRN_CHUNK = 64
HGRN_SAMPLE_ROWS = 16
HGRN_SAMPLE_GROUP = 4
ATTN_TQ = 128
ATTN_TK = 512
ATTN_ROW_CHUNKS = 4
SAMPLE_SEQS = 4
SAMPLE_PAGES = 4
SAMPLE_BUFS = 4
VMEM_LIMIT = 60 * 1024 * 1024


def _sigmoid(x):
    return 1.0 / (1.0 + jnp.exp(-x))


def _rms(x, g):
    return x * lax.rsqrt(jnp.mean(x * x, axis=-1, keepdims=True) + EPS) * g


def _dot(a, b):
    return jnp.dot(a, b, preferred_element_type=F32)


def _dot_nt(a, b):
    return lax.dot_general(a, b, (((1,), (1,)), ((), ())), preferred_element_type=F32)


def _dot_tn(a, b):
    return lax.dot_general(a, b, (((0,), (0,)), ((), ())), preferred_element_type=F32)


def _div_pow2(x, n):
    shift = int(math.log2(n))
    assert 2 ** shift == n
    return x >> shift


def _pick_tile(sizes, candidates):
    for c in candidates:
        if all(n % c == 0 for n in sizes):
            return c
    raise ValueError(f"no tile in {candidates} divides {sizes}")


def _resident(shape):
    nd = len(shape)
    return pl.BlockSpec(shape, lambda *_: (0,) * nd, pipeline_mode=pl.Buffered(1))


def _split_specs(tm, width, npb):
    return [pl.BlockSpec((tm, width), lambda i: (jnp.minimum(i, npb - 1), 0)),
            pl.BlockSpec((tm, width), lambda i: (jnp.maximum(i - npb, 0), 0))]


def _ffn_kernel(*refs, nf, post):
    refs = list(refs)
    x_ref, g_ref, wg_ref, wu_ref, wd_ref = refs[:5]
    refs = refs[5:]
    g2_ref = refs.pop(0) if post else None
    o_ref = refs.pop(0)
    h2_ref = refs.pop(0) if post else None
    h_scr, acc_scr = refs
    f = pl.program_id(1)

    @pl.when(f == 0)
    def _():
        h_scr[...] = _rms(x_ref[...], g_ref[...]).astype(BF16)
        acc_scr[...] = jnp.zeros(acc_scr.shape, F32)

    h = h_scr[...]
    g = _dot(h, wg_ref[...])
    u = _dot(h, wu_ref[...])
    a = (g * _sigmoid(g) * u).astype(BF16)
    acc_scr[...] += _dot(a, wd_ref[...])

    @pl.when(f == nf - 1)
    def _():
        y = x_ref[...] + 0.5 * acc_scr[...]
        o_ref[...] = y
        if post:
            h2_ref[...] = _rms(y, g2_ref[...]).astype(BF16)


def _ffn(x, g, wg, wu, wd, g2=None):
    n, d = x.shape
    dff = wg.shape[1]
    tm = _pick_tile([n], (512, 256, 128))
    tf = _pick_tile([dff], (512, 256, 128))
    nf = dff // tf
    post = g2 is not None
    in_specs = [
        pl.BlockSpec((tm, d), lambda i, f: (i, 0)),
        pl.BlockSpec((1, d), lambda i, f: (0, 0)),
        pl.BlockSpec((d, tf), lambda i, f: (0, f)),
        pl.BlockSpec((d, tf), lambda i, f: (0, f)),
        pl.BlockSpec((tf, d), lambda i, f: (f, 0)),
    ]
    args = [x, g, wg, wu, wd]
    out_shape = [jax.ShapeDtypeStruct((n, d), F32)]
    out_specs = [pl.BlockSpec((tm, d), lambda i, f: (i, 0))]
    if post:
        in_specs.append(pl.BlockSpec((1, d), lambda i, f: (0, 0)))
        args.append(g2)
        out_shape.append(jax.ShapeDtypeStruct((n, d), BF16))
        out_specs.append(pl.BlockSpec((tm, d), lambda i, f: (i, 0)))
    res = pl.pallas_call(
        functools.partial(_ffn_kernel, nf=nf, post=post),
        grid=(n // tm, nf),
        in_specs=in_specs,
        out_specs=out_specs,
        out_shape=out_shape,
        scratch_shapes=[pltpu.VMEM((tm, d), BF16), pltpu.VMEM((tm, d), F32)],
        compiler_params=pltpu.CompilerParams(
            dimension_semantics=("parallel", "arbitrary"), vmem_limit_bytes=VMEM_LIMIT),
        name="ffn_post" if post else "ffn",
    )(*args)
    return res if post else res[0]


def _pair_tile(pair, candidates=(512, 256, 128)):
    ptok, stok = pair[0].shape[0], pair[1].shape[0]
    tm = _pick_tile([ptok, stok], candidates)
    return tm, ptok // tm, (ptok + stok) // tm


def _pair_rows(npb, p_ref, s_ref):
    return jnp.where(pl.program_id(0) < npb, p_ref[...], s_ref[...])


def _gates_kernel(hp_ref, hs_ref, w_ref, o_ref, *, tn, npb):
    h = _pair_rows(npb, hp_ref, hs_ref)
    for j in range(w_ref.shape[0] // tn):
        cols = slice(j * tn, (j + 1) * tn)
        o_ref[:, cols] = _sigmoid(_dot_nt(h, w_ref[cols, :])).astype(BF16)


def _gates(h, wt):
    d = h[0].shape[1]
    nc = wt.shape[0]
    tm, npb, nblk = _pair_tile(h)
    tn = _pick_tile([nc], (1024, 512, 256, 128))
    return pl.pallas_call(
        functools.partial(_gates_kernel, tn=tn, npb=npb),
        grid=(nblk,),
        in_specs=_split_specs(tm, d, npb) + [_resident(wt.shape)],
        out_specs=pl.BlockSpec((tm, nc), lambda i: (i, 0)),
        out_shape=jax.ShapeDtypeStruct((nblk * tm, nc), BF16),
        compiler_params=pltpu.CompilerParams(
            dimension_semantics=("parallel",), vmem_limit_bytes=VMEM_LIMIT),
        name="gates",
    )(h[0], h[1], wt)


def _hgrn_proj_kernel(hp_ref, hs_ref, w_ref, lbl_ref, hv_ref, lf_ref, *, layer, c, npb):
    h = _pair_rows(npb, hp_ref, hs_ref)

    def proj(j):
        return _dot_nt(h, w_ref[j * c:(j + 1) * c, :])

    for j in range(2):
        hv_ref[:, j * c:(j + 1) * c] = proj(j).astype(BF16)
    y = proj(2)
    hv_ref[:, 2 * c:3 * c] = (y * _sigmoid(y)).astype(BF16)
    y = proj(3)
    logits = lbl_ref[...]
    e = jnp.exp(logits - jnp.max(logits, axis=0, keepdims=True))
    lb = jnp.sum(e[:layer + 1], axis=0, keepdims=True) / jnp.sum(e, axis=0, keepdims=True)
    lf_ref[...] = jnp.log(lb + (1.0 - lb) * _sigmoid(y))
    hv_ref[:, 3 * c:4 * c] = ((1.0 - lb) * _sigmoid(-y)).astype(BF16)


def _hgrn_proj(h, wt, lb_logits, layer):
    d = h[0].shape[1]
    c = wt.shape[0] // 4
    tm, npb, nblk = _pair_tile(h)
    n = nblk * tm
    return pl.pallas_call(
        functools.partial(_hgrn_proj_kernel, layer=layer, c=c, npb=npb),
        grid=(nblk,),
        in_specs=_split_specs(tm, d, npb) + [_resident(wt.shape), _resident(lb_logits.shape)],
        out_specs=[pl.BlockSpec((tm, 4 * c), lambda i: (i, 0)),
                   pl.BlockSpec((tm, c), lambda i: (i, 0))],
        out_shape=[jax.ShapeDtypeStruct((n, 4 * c), BF16),
                   jax.ShapeDtypeStruct((n, c), F32)],
        compiler_params=pltpu.CompilerParams(
            dimension_semantics=("parallel",), vmem_limit_bytes=VMEM_LIMIT),
        name="hgrn_proj",
    )(h[0], h[1], wt, lb_logits)


def _mla_prep_kernel(hp_ref, hs_ref, wm_ref, qn_ref, kvn_ref, wuq_ref, wuk_ref, cs_ref,
                     qc_ref, kc_ref, ckvp_ref, ckvs_ref, kpep_ref, kpes_ref,
                     *, heads, q_lora, kv_lora, nope, rope, scale, npb):
    i = pl.program_id(0)
    tm = hp_ref.shape[0]
    c = _dot_nt(_pair_rows(npb, hp_ref, hs_ref), wm_ref[...])
    cos = cs_ref[:, :LANES]
    sin = cs_ref[:, LANES:]
    ckv = _rms(c[:, q_lora:q_lora + kv_lora], kvn_ref[...])
    o = q_lora + kv_lora
    kpe = c[:, o:o + LANES] * cos + c[:, o + LANES:o + 2 * LANES] * sin
    kc_ref[:, :kv_lora] = ckv.astype(BF16)
    kc_ref[:, kv_lora:] = kpe.astype(BF16)

    @pl.when(i < npb)
    def _():
        ckvp_ref[...] = ckv
        kpep_ref[...] = kpe[:, :rope]

    @pl.when(i >= npb)
    def _():
        ckvs_ref[...] = ckv
        kpes_ref[...] = kpe[:, :rope]

    qn = _rms(c[:, :q_lora], qn_ref[...]).astype(BF16)
    q3 = _dot(qn, wuq_ref[...]) * scale
    nb = tm // LANES
    for h in range(heads):
        qh = q3[:, h * nope:(h + 1) * nope].astype(BF16)
        lat = _dot(qh, wuk_ref[h]).astype(BF16)
        qc_ref[:, h, :, :kv_lora] = lat.reshape(nb, LANES, kv_lora)
        r0 = heads * nope + h * LANES
        r1 = heads * nope + heads * LANES + h * LANES
        pe = (q3[:, r0:r0 + LANES] * cos + q3[:, r1:r1 + LANES] * sin).astype(BF16)
        qc_ref[:, h, :, kv_lora:] = pe.reshape(nb, LANES, LANES)


def _mla_prep(h, w_mlat, q_norm, kv_norm, w_uq3, w_ukt, cs, *, heads, q_lora, kv_lora, nope, rope, scale):
    d = h[0].shape[1]
    ptok, stok = h[0].shape[0], h[1].shape[0]
    tm, npb, nblk = _pair_tile(h)
    n = nblk * tm
    nb = tm // LANES
    cw = kv_lora + LANES
    kern = functools.partial(_mla_prep_kernel, heads=heads, q_lora=q_lora, kv_lora=kv_lora,
                             nope=nope, rope=rope, scale=scale, npb=npb)
    return pl.pallas_call(
        kern,
        grid=(nblk,),
        in_specs=_split_specs(tm, d, npb)
                 + [_resident(w_mlat.shape), _resident(q_norm.shape), _resident(kv_norm.shape),
                    _resident(w_uq3.shape), _resident(w_ukt.shape),
                    pl.BlockSpec((tm, 2 * LANES), lambda i: (i, 0))],
        out_specs=[pl.BlockSpec((nb, heads, LANES, cw), lambda i: (i, 0, 0, 0)),
                   pl.BlockSpec((tm, cw), lambda i: (i, 0))]
                  + _split_specs(tm, kv_lora, npb) + _split_specs(tm, rope, npb),
        out_shape=[jax.ShapeDtypeStruct((n // LANES, heads, LANES, cw), BF16),
                   jax.ShapeDtypeStruct((n, cw), BF16),
                   jax.ShapeDtypeStruct((ptok, kv_lora), F32),
                   jax.ShapeDtypeStruct((stok, kv_lora), F32),
                   jax.ShapeDtypeStruct((ptok, rope), F32),
                   jax.ShapeDtypeStruct((stok, rope), F32)],
        compiler_params=pltpu.CompilerParams(
            dimension_semantics=("arbitrary",), vmem_limit_bytes=VMEM_LIMIT),
        name="mla_prep",
    )(h[0], h[1], w_mlat, q_norm, kv_norm, w_uq3, w_ukt, cs)


def _attn_prompt_kernel(q_ref, k_ref, wuv_ref, o_ref, m_scr, l_scr, acc_scr,
                        *, tq, tk, heads, vdim, c, nchunk):
    i = pl.program_id(1)
    hpc = heads // nchunk
    rc = hpc * tq
    m_scr[...] = jnp.full(m_scr.shape, -jnp.inf, F32)
    l_scr[...] = jnp.zeros(l_scr.shape, F32)
    acc_scr[...] = jnp.zeros(acc_scr.shape, F32)

    assert tq == LANES and tk % tq == 0

    def step(off, width, masked):
        nt = width // LANES
        k = k_ref[pl.ds(off, width), :]
        v = k[:, :c]
        if masked:
            t_row = lax.broadcasted_iota(jnp.int32, (rc, 1), 0) & (tq - 1)
            keep = lax.broadcasted_iota(jnp.int32, (1, tq), 1) <= t_row

        def scores(r):
            q = q_ref[0, r * hpc:(r + 1) * hpc].reshape(rc, q_ref.shape[-1])
            s = _dot_nt(q, k)
            if masked:
                last = jnp.where(keep, s[:, width - tq:], -jnp.inf)
                s = last if nt == 1 else jnp.concatenate([s[:, :width - tq], last], axis=1)
            return s

        def update(r, s):
            rs = slice(r * rc, (r + 1) * rc)
            m_prev = m_scr[rs]
            smax = s[:, :LANES]
            for t in range(1, nt):
                smax = jnp.maximum(smax, s[:, t * LANES:(t + 1) * LANES])
            m_new = jnp.maximum(m_prev, jnp.max(smax, axis=-1, keepdims=True))
            alpha = jnp.exp2(m_prev - m_new)
            p = jnp.exp2(s - jnp.concatenate([m_new] * nt, axis=1))
            psum = p[:, :LANES]
            for t in range(1, nt):
                psum = psum + p[:, t * LANES:(t + 1) * LANES]
            l_scr[rs] = alpha * l_scr[rs] + psum
            acc_scr[rs] = jnp.concatenate([alpha] * (c // LANES), axis=1) * acc_scr[rs] + _dot(p.astype(BF16), v)
            m_scr[rs] = m_new

        s_cur = scores(0)
        for r in range(nchunk):
            s_next = scores(r + 1) if r + 1 < nchunk else None
            update(r, s_cur)
            s_cur = s_next

    ratio = tk // tq
    nfull = i // ratio

    def body(j, carry):
        step(pl.multiple_of(j * tk, tk), tk, False)
        return carry

    lax.fori_loop(0, nfull, body, 0)
    for rem in range(ratio):
        @pl.when(i % ratio == rem)
        def _():
            step(pl.multiple_of(nfull * tk, tk), (rem + 1) * tq, True)

    inv = 1.0 / jnp.sum(l_scr[...], axis=-1, keepdims=True)
    for h in range(heads):
        oh = (acc_scr[h * tq:(h + 1) * tq, :] * inv[h * tq:(h + 1) * tq]).astype(BF16)
        o_ref[:, h * vdim:(h + 1) * vdim] = _dot(oh, wuv_ref[:, h * vdim:(h + 1) * vdim]).astype(BF16)


def _attn_prompt(qc, kc, wuv, bsz, seq, *, heads, vdim, c):
    cw = kc.shape[1]
    tq = ATTN_TQ
    tk = min(ATTN_TK, seq)
    nq = seq // tq
    rows = heads * tq
    nchunk = ATTN_ROW_CHUNKS
    kern = functools.partial(_attn_prompt_kernel, tq=tq, tk=tk, heads=heads, vdim=vdim, c=c, nchunk=nchunk)
    return pl.pallas_call(
        kern,
        grid=(bsz, nq),
        in_specs=[pl.BlockSpec((1, heads, tq, cw), lambda bi, i: (bi * nq + i, 0, 0, 0)),
                  pl.BlockSpec((seq, cw), lambda bi, i: (bi, 0)),
                  _resident(wuv.shape)],
        out_specs=pl.BlockSpec((tq, heads * vdim), lambda bi, i: (bi * nq + i, 0)),
        out_shape=jax.ShapeDtypeStruct((bsz * seq, heads * vdim), BF16),
        scratch_shapes=[pltpu.VMEM((rows, LANES), F32), pltpu.VMEM((rows, LANES), F32),
                        pltpu.VMEM((rows, c), F32)],
        compiler_params=pltpu.CompilerParams(
            dimension_semantics=("parallel", "arbitrary"), vmem_limit_bytes=VMEM_LIMIT),
        name="attn_prompt",
    )(qc, kc, wuv)


def _attn_sample_kernel(pt_ref, q_ref, kn_ref, wuv_ref, cck_ref, cpe_ref, o_ref,
                        bufk, bufp, sem, m_scr, l_scr, acc_scr,
                        *, nseq, nsq, npages, ch, nbuf, heads, vdim, ntok, c, rope):
    cps = npages // ch
    total = (nseq // nsq) * cps

    def unit_copies(unit, slot):
        pair = unit // cps
        part = unit % cps
        copies = []
        for sq in range(nsq):
            base = (nsq * pair + sq) * npages + part * ch
            for pg in range(ch):
                page = pt_ref[base + pg]
                copies.append(pltpu.make_async_copy(
                    cck_ref.at[page], bufk.at[slot, sq, pl.ds(pg * PAGE_SIZE, PAGE_SIZE), :],
                    sem.at[0, slot]))
                copies.append(pltpu.make_async_copy(
                    cpe_ref.at[page], bufp.at[slot, sq, :, pl.ds(pg * PAGE_SIZE, PAGE_SIZE)],
                    sem.at[1, slot]))
        return copies

    def start_unit(unit):
        for cp in unit_copies(unit, unit % nbuf):
            cp.start()

    def wait_unit(unit):
        for cp in unit_copies(unit, unit % nbuf):
            cp.wait()

    for u0 in range(min(nbuf - 1, total)):
        start_unit(u0)

    def softmax_update(sq, s, v):
        m_prev = m_scr[sq]
        m_new = jnp.maximum(m_prev, jnp.max(s, axis=-1, keepdims=True))
        alpha = jnp.exp2(m_prev - m_new)
        p = jnp.exp2(s - m_new)
        l_scr[sq] = alpha * l_scr[sq] + jnp.sum(p, axis=-1, keepdims=True)
        acc_scr[sq] = alpha * acc_scr[sq] + _dot(p.astype(BF16), v)
        m_scr[sq] = m_new

    def body(u, carry):
        pair = u // cps
        part = u % cps
        slot = u % nbuf

        @pl.when(u + nbuf - 1 < total)
        def _():
            start_unit(u + nbuf - 1)

        wait_unit(u)

        @pl.when(part == 0)
        def _():
            m_scr[...] = jnp.full(m_scr.shape, -jnp.inf, F32)
            l_scr[...] = jnp.zeros(l_scr.shape, F32)
            acc_scr[...] = jnp.zeros(acc_scr.shape, F32)

        kbs, ss = [], []
        for sq in range(nsq):
            q = q_ref[nsq * pair + sq]
            kb = bufk[slot, sq].astype(BF16)
            kpt = bufp[slot, sq].astype(BF16)
            kbs.append(kb)
            ss.append(_dot_nt(q[:, :c], kb) + _dot(q[:, c:c + rope], kpt))
        for sq in range(nsq):
            softmax_update(sq, ss[sq], kbs[sq])

        @pl.when(part == cps - 1)
        def _():
            for sq in range(nsq):
                seq = nsq * pair + sq
                q = q_ref[seq]
                kn = kn_ref[seq]
                sn = _dot_nt(q, kn)
                rows, cols = sn.shape
                t_row = _div_pow2(lax.broadcasted_iota(jnp.int32, (rows, 1), 0), heads)
                col = lax.broadcasted_iota(jnp.int32, (1, cols), 1)
                sn = jnp.where((col <= t_row) & (col < ntok), sn, -jnp.inf)
                softmax_update(sq, sn, kn[:, :c])
                o = (acc_scr[sq] / l_scr[sq]).astype(BF16)
                y = _dot(o, wuv_ref[...])
                h_row = lax.broadcasted_iota(jnp.int32, (rows, 1), 0) & (heads - 1)
                h_col = _div_pow2(lax.broadcasted_iota(jnp.int32, (1, heads * vdim), 1), vdim)
                z = jnp.where(h_row == h_col, y, 0.0)
                tid = lax.broadcasted_iota(jnp.int32, (ntok, 1), 0)
                out = jnp.zeros((ntok, heads * vdim), F32)
                for t in range(ntok):
                    zt = jnp.sum(z[t * heads:(t + 1) * heads], axis=0, keepdims=True)
                    out = jnp.where(tid == t, zt, out)
                o_ref[seq] = out

        return carry

    lax.fori_loop(0, total, body, 0)


def _attn_sample(page_table, qs, kn, wuv, cache_ckv, cache_kpet, *, heads, vdim, ntok, c):
    nseq, rows, _ = qs.shape
    npages = page_table.shape[1]
    rope = cache_kpet.shape[1]
    ch = min(SAMPLE_PAGES, npages)
    nsq = SAMPLE_SEQS
    assert npages % ch == 0 and nseq % nsq == 0
    nbuf = SAMPLE_BUFS
    kern = functools.partial(_attn_sample_kernel, nseq=nseq, nsq=nsq, npages=npages, ch=ch, nbuf=nbuf,
                             heads=heads, vdim=vdim, ntok=ntok, c=c, rope=rope)
    vmem = pl.BlockSpec(memory_space=pltpu.VMEM)
    return pl.pallas_call(
        kern,
        in_specs=[pl.BlockSpec(memory_space=pltpu.SMEM), vmem, vmem, vmem,
                  pl.BlockSpec(memory_space=pl.ANY), pl.BlockSpec(memory_space=pl.ANY)],
        out_specs=vmem,
        out_shape=jax.ShapeDtypeStruct((nseq, ntok, heads * vdim), F32),
        scratch_shapes=[pltpu.VMEM((nbuf, nsq, ch * PAGE_SIZE, c), F32),
                        pltpu.VMEM((nbuf, nsq, rope, ch * PAGE_SIZE), F32),
                        pltpu.SemaphoreType.DMA((2, nbuf)),
                        pltpu.VMEM((nsq, rows, 1), F32), pltpu.VMEM((nsq, rows, 1), F32),
                        pltpu.VMEM((nsq, rows, c), F32)],
        compiler_params=pltpu.CompilerParams(vmem_limit_bytes=VMEM_LIMIT),
        name="attn_sample",
    )(page_table.reshape(-1), qs, kn, wuv, cache_ckv, cache_kpet)


def _hgrn_kernel(hv_ref, lf_ref, s0_ref, gn_ref, ob_ref, so_ref,
                 s_scr, q_scr, k_scr, b_scr, at_scr,
                 *, L, G, sub, n_t, heads, dk, dv):
    c = pl.program_id(1)
    nc = pl.num_programs(1)

    @pl.when(c == 0)
    def _():
        s_scr[...] = s0_ref[...]

    for gi in range(G):
        _hgrn_chunk(hv_ref, lf_ref, gn_ref, ob_ref, s_scr, q_scr, k_scr, b_scr, at_scr,
                    gi, L=L, sub=sub, n_t=n_t, heads=heads, dk=dk, dv=dv)

    @pl.when(c == nc - 1)
    def _():
        so_ref[...] = s_scr[...]


def _hgrn_chunk(hv_ref, lf_ref, gn_ref, ob_ref, s_scr, q_scr, k_scr, b_scr, at_scr,
                gi, *, L, sub, n_t, heads, dk, dv):
    hd = heads * dk
    nsub = L // sub
    levels = int(math.log2(nsub))
    assert 2 ** levels == nsub
    rows = slice(gi * L, (gi + 1) * L)

    g = lf_ref[rows, :]
    g1 = g.astype(BF16)
    r1 = g - g1.astype(F32)
    g2 = r1.astype(BF16)
    g3 = (r1 - g2.astype(F32)).astype(BF16)
    row = lax.broadcasted_iota(jnp.int32, (L, L), 0)
    col = lax.broadcasted_iota(jnp.int32, (L, L), 1)
    tril = jnp.where(col <= row, 1.0, 0.0).astype(BF16)
    b = _dot(tril, g1) + _dot(tril, g2) + _dot(tril, g3)

    def group_bounds(grp):
        ends = [b[(j + 1) * grp - 1:(j + 1) * grp, :] for j in range(L // grp)]
        starts = [jnp.zeros_like(ends[0])] + ends[:-1]
        rep = lambda rows_: jnp.concatenate([jnp.broadcast_to(r, (grp, hd)) for r in rows_], axis=0)
        return rep(starts), rep(ends)

    bstart, bend = group_bounds(sub)
    brel = b - bstart
    blast = b[L - 1:L, :]

    q = hv_ref[rows, 0:hd].astype(F32)
    v = hv_ref[rows, hd:2 * hd]
    sg = hv_ref[rows, 2 * hd:3 * hd].astype(F32)
    k = hv_ref[rows, 3 * hd:4 * hd].astype(F32)

    q_scr[rows, :] = q
    k_scr[rows, :] = k
    b_scr[rows, :] = brel * math.log2(math.e)

    qt = q * jnp.exp(brel)
    kh = k * jnp.exp(bend - b)
    q_in = (qt * jnp.exp(bstart)).astype(BF16)
    k_st = (kh * jnp.exp(blast - bend)).astype(BF16)
    e_last = jnp.exp(blast)

    lhs = [qt.astype(BF16)]
    rhs = [kh.astype(BF16)]
    valid = []
    for lv in range(levels):
        grp = sub * (2 ** lv)
        if lv > 0:
            gs, ge = group_bounds(grp)
            lhs.append((qt * jnp.exp(bstart - gs)).astype(BF16))
            rhs.append((kh * jnp.exp(ge - bend)).astype(BF16))
        cg = _div_pow2(col, grp)
        valid.append(((cg & 1) == 1) & (_div_pow2(row, grp) == cg - 1))

    rid8 = lax.broadcasted_iota(jnp.int32, (8, 1), 0)
    lane = lax.broadcasted_iota(jnp.int32, (8, dk), 1)
    lss = [slice(h * dk, (h + 1) * dk) for h in range(heads)]
    halves = sub // 8

    def diag_body(i, carry):
        r0 = pl.multiple_of(gi * L + i * sub, sub)
        at = [[jnp.zeros((8, dk), F32) for _ in range(halves)] for _ in range(heads)]
        blk = [[ref[pl.ds(r0, sub), ls] for ref in (q_scr, k_scr, b_scr)] for ls in lss]
        for t in range(n_t):
            is_col = lane == i * sub + t
            for h in range(heads):
                qb, kb, bb = blk[h]
                for hf in range(t // 8 + 1):
                    rr = slice(8 * hf, 8 * hf + 8)
                    w = jnp.exp2(bb[t:t + 1, :] - bb[rr]) * kb[rr] * qb[t:t + 1, :]
                    a = jnp.sum(w, axis=-1, keepdims=True)
                    if t < 8 * hf + 7:
                        a = jnp.where(rid8 + 8 * hf <= t, a, 0.0)
                    at[h][hf] = jnp.where(is_col, a, at[h][hf])
        for h, ls in enumerate(lss):
            for hf in range(halves):
                at_scr[pl.ds(pl.multiple_of(r0 + 8 * hf, 8), 8), ls] = at[h][hf]
        return carry

    lax.fori_loop(0, nsub, diag_body, 0)

    gn = gn_ref[...]
    s_prev = [s_scr[gi, h] for h in range(heads)]
    o_in = [_dot(q_in[:, ls], s_prev[h].astype(BF16)) for h, ls in enumerate(lss)]
    p_off = [[_dot_nt(rhs[lv][:, ls], lhs[lv][:, ls]) for lv in range(levels)] for ls in lss]
    s_upd = [_dot_tn(k_st[:, ls], v[:, ls]) for ls in lss]
    o_mix = []
    for h, ls in enumerate(lss):
        if levels > 0:
            a_t = at_scr[rows, h * dk:h * dk + L]
            for lv in range(levels):
                a_t = a_t + jnp.where(valid[lv], p_off[h][lv], 0.0)
            o_mix.append(_dot_tn(a_t.astype(BF16), v[:, ls]))
        else:
            o_mix.append(_dot_tn(at_scr[rows, ls].astype(BF16), v[:, ls])[:L])
    for h, ls in enumerate(lss):
        decay = jnp.transpose(jnp.broadcast_to(e_last[:, ls], (dk, dk)))
        s_scr[gi, h] = decay * s_prev[h] + s_upd[h]
        o = o_in[h] + o_mix[h]
        on = o * lax.rsqrt(jnp.mean(o * o, axis=-1, keepdims=True) + EPS) * gn
        ob_ref[rows, ls] = (on * sg[:, ls]).astype(BF16)


def _hgrn(hv, lf, s0, gn, nc, *, L, G, sub, n_t, heads, dk, dv):
    nseq = s0.shape[0]
    rows = nseq * nc * L
    hd = heads * dk
    assert nseq % G == 0 and (G == 1 or nc == 1)
    kern = functools.partial(_hgrn_kernel, L=L, G=G, sub=sub, n_t=n_t, heads=heads, dk=dk, dv=dv)
    return pl.pallas_call(
        kern,
        grid=(nseq // G, nc),
        in_specs=[pl.BlockSpec((G * L, 4 * hd), lambda s, c: (s * nc + c, 0)),
                  pl.BlockSpec((G * L, hd), lambda s, c: (s * nc + c, 0)),
                  pl.BlockSpec((G, heads, dk, dv), lambda s, c: (s, 0, 0, 0)),
                  pl.BlockSpec((1, dv), lambda s, c: (0, 0))],
        out_specs=[pl.BlockSpec((G * L, heads * dv), lambda s, c: (s * nc + c, 0)),
                   pl.BlockSpec((G, heads, dk, dv), lambda s, c: (s, 0, 0, 0))],
        out_shape=[jax.ShapeDtypeStruct((rows, heads * dv), BF16),
                   jax.ShapeDtypeStruct((nseq, heads, dk, dv), F32)],
        scratch_shapes=[pltpu.VMEM((G, heads, dk, dv), F32)] + [pltpu.VMEM((G * L, hd), F32)] * 4,
        compiler_params=pltpu.CompilerParams(
            dimension_semantics=("parallel", "arbitrary"), vmem_limit_bytes=VMEM_LIMIT),
        name="hgrn",
    )(hv, lf, s0, gn)


def _merge_kernel(oap_ref, oas_ref, obp_ref, obs_ref, xp_ref, xs_ref, sg_ref, wa_ref, wb_ref, wo_ref, o_ref,
                  *, d, npb):
    a = _dot(_pair_rows(npb, oap_ref, oas_ref), wa_ref[...])
    b = _dot(_pair_rows(npb, obp_ref, obs_ref), wb_ref[...])
    merged = sg_ref[:, :d].astype(F32) * a + sg_ref[:, d:].astype(F32) * b
    o_ref[...] = _pair_rows(npb, xp_ref, xs_ref) + _dot(merged.astype(BF16), wo_ref[...])


def _merge(oa, ob, x, sg, wa, wb, wo):
    d = x[0].shape[1]
    tm, npb, nblk = _pair_tile(x, (256, 128))
    return pl.pallas_call(
        functools.partial(_merge_kernel, d=d, npb=npb),
        grid=(nblk,),
        in_specs=_split_specs(tm, oa[0].shape[1], npb) + _split_specs(tm, ob[0].shape[1], npb)
                 + _split_specs(tm, d, npb)
                 + [pl.BlockSpec((tm, 2 * d), lambda i: (i, 0)),
                    _resident(wa.shape), _resident(wb.shape), _resident(wo.shape)],
        out_specs=pl.BlockSpec((tm, d), lambda i: (i, 0)),
        out_shape=jax.ShapeDtypeStruct((nblk * tm, d), F32),
        compiler_params=pltpu.CompilerParams(
            dimension_semantics=("parallel",), vmem_limit_bytes=VMEM_LIMIT),
        name="merge_out",
    )(oa[0], oa[1], ob[0], ob[1], x[0], x[1], sg, wa, wb, wo)


def _ple_kernel(x_ref, pp_ref, ps_ref, g_ref, wg_ref, wp_ref, gf_ref, yp_ref, ys_ref, *, final, npb):
    i = pl.program_id(0)
    x = x_ref[...]
    gate = _sigmoid(_dot(_rms(x, g_ref[...]).astype(BF16), wg_ref[...]))
    p = jnp.where(i < npb, pp_ref[...], ps_ref[...])
    y = x + gate * _dot(p.astype(BF16), wp_ref[...])
    if final:
        y = _rms(y, gf_ref[...])

    @pl.when(i < npb)
    def _():
        yp_ref[...] = y

    @pl.when(i >= npb)
    def _():
        ys_ref[...] = y


def _ple(x, pp, ps, g, wg, wp, gf, final):
    n, d = x.shape
    ptok, stok = pp.shape[0], ps.shape[0]
    tm = _pick_tile([ptok, stok], (512, 256, 128))
    npb = ptok // tm
    return pl.pallas_call(
        functools.partial(_ple_kernel, final=final, npb=npb),
        grid=(n // tm,),
        in_specs=[pl.BlockSpec((tm, d), lambda i: (i, 0))] + _split_specs(tm, pp.shape[1], npb)
                 + [_resident(g.shape), _resident(wg.shape), _resident(wp.shape), _resident(gf.shape)],
        out_specs=_split_specs(tm, d, npb),
        out_shape=[jax.ShapeDtypeStruct((ptok, d), F32), jax.ShapeDtypeStruct((stok, d), F32)],
        compiler_params=pltpu.CompilerParams(
            dimension_semantics=("arbitrary",), vmem_limit_bytes=VMEM_LIMIT),
        name="ple",
    )(x, pp, ps, g, wg, wp, gf)


def _rot_last(w):
    half = w.shape[-1] // 2
    return jnp.concatenate([-w[..., half:], w[..., :half]], axis=-1)


def _pad_to_lanes(w, axis):
    pads = [(0, 0)] * w.ndim
    pads[axis] = (0, LANES - w.shape[axis])
    return jnp.pad(w, pads)


def kernel(x_prompt, x_sample, cache_ckv, cache_kpe, state_hgrn, page_table, p_prompt, p_sample, ffn1_norm, ffn1_w_gate, ffn1_w_up, ffn1_w_down, mix_norm, w_in, q_norm, w_uq, kv_norm, w_uk, w_uv, hgrn_lb_logits, hgrn_out_norm, w_branch_a, w_branch_b, w_out, ffn2_norm, ffn2_w_gate, ffn2_w_up, ffn2_w_down, ple_norm, w_ple_gate, w_ple_proj, final_norm):
    bsz, seq, d = x_prompt.shape
    nseq, ntok, _ = x_sample.shape
    depth = w_in.shape[0]
    q_lora = q_norm.shape[1]
    kv_lora, heads, nope = w_uk.shape[1:]
    vdim = w_uv.shape[-1]
    rope = cache_kpe.shape[-1]
    hb, dk, dv = state_hgrn.shape[2:]
    hd = hb * dk
    npages = page_table.shape[1]
    past_len = npages * PAGE_SIZE
    ptok = bsz * seq
    stok = nseq * ntok
    scale = math.log2(math.e) / math.sqrt(nope + rope)
    assert rope <= LANES and nope == LANES and dk == LANES and dv == LANES

    x = (x_prompt.reshape(ptok, d), x_sample.reshape(stok, d))

    half = rope // 2
    inv = ROPE_THETA ** (-jnp.arange(half, dtype=F32) / half)
    pos = jnp.concatenate([jnp.tile(jnp.arange(seq, dtype=F32), bsz),
                           jnp.tile(jnp.arange(ntok, dtype=F32) + past_len, nseq)])
    ang = pos[:, None] * inv[None, :]
    cos = jnp.cos(ang)
    sin = jnp.sin(ang)
    cs = jnp.concatenate([_pad_to_lanes(jnp.concatenate([cos, cos], axis=1), 1),
                          _pad_to_lanes(jnp.concatenate([sin, sin], axis=1), 1)], axis=1)

    sp = [0]
    for w in (q_lora, kv_lora, rope, hd, hd, hb * dv, hb * dv, d, d):
        sp.append(sp[-1] + w)

    ckv_p, kpe_p, st_p, ckv_s, kpe_s, st_s = [], [], [], [], [], []
    for i in range(depth):
        wit = jnp.swapaxes(w_in[i], 0, 1)
        seg = [wit[sp[j]:sp[j + 1]] for j in range(9)]
        kr_t = seg[2]
        kr_rot_t = jnp.swapaxes(_rot_last(jnp.swapaxes(kr_t, 0, 1)), 0, 1)
        w_mlat = jnp.concatenate([seg[0], seg[1], _pad_to_lanes(kr_t, 0), _pad_to_lanes(kr_rot_t, 0)],
                                 axis=0).astype(BF16)
        w_hgt = jnp.concatenate([seg[3], seg[5], seg[6], seg[4]], axis=0).astype(BF16)
        w_gtt = jnp.concatenate([seg[7], seg[8]], axis=0).astype(BF16)
        wq = w_uq[i].reshape(q_lora, heads, nope + rope)
        wq_rope = wq[:, :, nope:]
        w_uq3 = jnp.concatenate([wq[:, :, :nope].reshape(q_lora, heads * nope),
                                 _pad_to_lanes(wq_rope, 2).reshape(q_lora, heads * LANES),
                                 _pad_to_lanes(_rot_last(wq_rope), 2).reshape(q_lora, heads * LANES)],
                                axis=1).astype(BF16)
        w_ukt = jnp.transpose(w_uk[i], (1, 2, 0)).astype(BF16)
        w_uv2 = w_uv[i].reshape(kv_lora, heads * vdim).astype(BF16)

        ffn1 = (ffn1_norm[i][None], ffn1_w_gate[i].astype(BF16), ffn1_w_up[i].astype(BF16),
                ffn1_w_down[i].astype(BF16), mix_norm[i][None])
        (x_p, hmix_p), (x_s, hmix_s) = _ffn(x[0], *ffn1), _ffn(x[1], *ffn1)
        x, hmix = (x_p, x_s), (hmix_p, hmix_s)

        sg = _gates(hmix, w_gtt)
        hv, lf = _hgrn_proj(hmix, w_hgt, hgrn_lb_logits, i)
        qc, kc, ckv_fp, ckv_fs, kpe_fp, kpe_fs = _mla_prep(
            hmix, w_mlat, q_norm[i][None], kv_norm[i][None], w_uq3, w_ukt, cs,
            heads=heads, q_lora=q_lora, kv_lora=kv_lora, nope=nope, rope=rope, scale=scale)

        npb = ptok // LANES
        oa_p = _attn_prompt(qc, kc, w_uv2, bsz, seq, heads=heads, vdim=vdim, c=kv_lora)
        cw = qc.shape[-1]
        qs = jnp.transpose(qc[npb:], (0, 2, 1, 3)).reshape(nseq, ntok * heads, cw)
        kn = jnp.pad(kc[ptok:].reshape(nseq, ntok, cw), ((0, 0), (0, 16 - ntok), (0, 0)))
        oa_s = _attn_sample(page_table, qs, kn, w_uv2, cache_ckv[i], jnp.swapaxes(cache_kpe[i], 1, 2),
                            heads=heads, vdim=vdim, ntok=ntok, c=kv_lora)
        oa = (oa_p, oa_s.reshape(stok, heads * vdim).astype(BF16))

        gn = hgrn_out_norm[i][None]
        ob_p, s_p = _hgrn(hv, lf, jnp.zeros((bsz, hb, dk, dv), F32), gn, seq // HGRN_CHUNK,
                          L=HGRN_CHUNK, G=1, sub=HGRN_SUB, n_t=HGRN_SUB, heads=hb, dk=dk, dv=dv)
        srows = HGRN_SAMPLE_ROWS
        pad = srows - ntok
        hv_s = jnp.pad(hv[ptok:].reshape(nseq, ntok, 4 * hd), ((0, 0), (0, pad), (0, 0)))
        lf_s = jnp.pad(lf[ptok:].reshape(nseq, ntok, hd), ((0, 0), (0, pad), (0, 0)))
        ob_s, s_s = _hgrn(hv_s.reshape(nseq * srows, 4 * hd), lf_s.reshape(nseq * srows, hd),
                          state_hgrn[i], gn, 1, L=srows, G=HGRN_SAMPLE_GROUP, sub=srows, n_t=ntok,
                          heads=hb, dk=dk, dv=dv)
        ob = (ob_p, ob_s.reshape(nseq, srows, hb * dv)[:, :ntok].reshape(stok, hb * dv))

        x = _merge(oa, ob, x, sg, w_branch_a[i].astype(BF16), w_branch_b[i].astype(BF16),
                   w_out[i].astype(BF16))
        x = _ffn(x, ffn2_norm[i][None], ffn2_w_gate[i].astype(BF16), ffn2_w_up[i].astype(BF16),
                 ffn2_w_down[i].astype(BF16))
        x = tuple(_ple(x, p_prompt[i].reshape(ptok, -1), p_sample[i].reshape(stok, -1), ple_norm[i][None],
                       w_ple_gate[i].astype(BF16), w_ple_proj[i].astype(BF16), final_norm[None],
                       i == depth - 1))

        ckv_p.append(ckv_fp.reshape(bsz, seq, kv_lora))
        kpe_p.append(kpe_fp.reshape(bsz, seq, rope))
        st_p.append(s_p)
        ckv_s.append(ckv_fs.reshape(nseq, ntok, kv_lora))
        kpe_s.append(kpe_fs.reshape(nseq, ntok, rope))
        st_s.append(s_s)

    return (x[0].reshape(bsz, seq, d), x[1].reshape(nseq, ntok, d),
            jnp.stack(ckv_p), jnp.stack(kpe_p), jnp.stack(st_p),
            jnp.stack(ckv_s), jnp.stack(kpe_s), jnp.stack(st_s))
```

```python
import functools
import math

import jax
import jax.numpy as jnp
from jax import lax
from jax.experimental import pallas as pl
from jax.experimental.pallas import tpu as pltpu

F32 = jnp.float32
BF16 = jnp.bfloat16

EPS = 1e-6
ROPE_THETA = 10000.0
PAGE_SIZE = 128
LANES = 128
HGRN_SUB = 16
HGRN_CHUNK = 64
HGRN_SAMPLE_ROWS = 16
HGRN_SAMPLE_GROUP = 8
ATTN_TQ = 128
ATTN_TK = 1024
ATTN_ROW_CHUNKS = 4
SAMPLE_SEQS = 4
SAMPLE_PAGES = 4
SAMPLE_BUFS = 4
VMEM_LIMIT = 60 * 1024 * 1024


def _sigmoid(x):
    return 1.0 / (1.0 + jnp.exp(-x))


def _rms(x, g):
    return x * lax.rsqrt(jnp.mean(x * x, axis=-1, keepdims=True) + EPS) * g


def _dot(a, b):
    return jnp.dot(a, b, preferred_element_type=F32)


def _dot_nt(a, b):
    return lax.dot_general(a, b, (((1,), (1,)), ((), ())), preferred_element_type=F32)


def _dot_tn(a, b):
    return lax.dot_general(a, b, (((0,), (0,)), ((), ())), preferred_element_type=F32)


def _div_pow2(x, n):
    shift = int(math.log2(n))
    assert 2 ** shift == n
    return x >> shift


def _pick_tile(sizes, candidates):
    for c in candidates:
        if all(n % c == 0 for n in sizes):
            return c
    raise ValueError(f"no tile in {candidates} divides {sizes}")


def _resident(shape):
    nd = len(shape)
    return pl.BlockSpec(shape, lambda *_: (0,) * nd, pipeline_mode=pl.Buffered(1))


def _split_specs(tm, width, npb):
    return [pl.BlockSpec((tm, width), lambda i: (jnp.minimum(i, npb - 1), 0)),
            pl.BlockSpec((tm, width), lambda i: (jnp.maximum(i - npb, 0), 0))]


def _ffn_kernel(*refs, nf, post):
    refs = list(refs)
    x_ref, g_ref, wg_ref, wu_ref, wd_ref = refs[:5]
    refs = refs[5:]
    g2_ref = refs.pop(0) if post else None
    o_ref = refs.pop(0)
    h2_ref = refs.pop(0) if post else None
    h_scr, acc_scr = refs
    f = pl.program_id(1)

    @pl.when(f == 0)
    def _():
        h_scr[...] = _rms(x_ref[...], g_ref[...]).astype(BF16)
        acc_scr[...] = jnp.zeros(acc_scr.shape, F32)

    h = h_scr[...]
    g = _dot(h, wg_ref[...])
    u = _dot(h, wu_ref[...])
    a = (g * _sigmoid(g) * u).astype(BF16)
    acc_scr[...] += _dot(a, wd_ref[...])

    @pl.when(f == nf - 1)
    def _():
        y = x_ref[...] + 0.5 * acc_scr[...]
        o_ref[...] = y
        if post:
            h2_ref[...] = _rms(y, g2_ref[...]).astype(BF16)


def _ffn(x, g, wg, wu, wd, g2=None):
    n, d = x.shape
    dff = wg.shape[1]
    tm = _pick_tile([n], (512, 256, 128))
    tf = _pick_tile([dff], (512, 256, 128))
    nf = dff // tf
    post = g2 is not None
    in_specs = [
        pl.BlockSpec((tm, d), lambda i, f: (i, 0)),
        pl.BlockSpec((1, d), lambda i, f: (0, 0)),
        pl.BlockSpec((d, tf), lambda i, f: (0, f)),
        pl.BlockSpec((d, tf), lambda i, f: (0, f)),
        pl.BlockSpec((tf, d), lambda i, f: (f, 0)),
    ]
    args = [x, g, wg, wu, wd]
    out_shape = [jax.ShapeDtypeStruct((n, d), F32)]
    out_specs = [pl.BlockSpec((tm, d), lambda i, f: (i, 0))]
    if post:
        in_specs.append(pl.BlockSpec((1, d), lambda i, f: (0, 0)))
        args.append(g2)
        out_shape.append(jax.ShapeDtypeStruct((n, d), BF16))
        out_specs.append(pl.BlockSpec((tm, d), lambda i, f: (i, 0)))
    res = pl.pallas_call(
        functools.partial(_ffn_kernel, nf=nf, post=post),
        grid=(n // tm, nf),
        in_specs=in_specs,
        out_specs=out_specs,
        out_shape=out_shape,
        scratch_shapes=[pltpu.VMEM((tm, d), BF16), pltpu.VMEM((tm, d), F32)],
        compiler_params=pltpu.CompilerParams(
            dimension_semantics=("parallel", "arbitrary"), vmem_limit_bytes=VMEM_LIMIT),
        name="ffn_post" if post else "ffn",
    )(*args)
    return res if post else res[0]


def _pair_tile(pair, candidates=(512, 256, 128)):
    ptok, stok = pair[0].shape[0], pair[1].shape[0]
    tm = _pick_tile([ptok, stok], candidates)
    return tm, ptok // tm, (ptok + stok) // tm


def _pair_rows(npb, p_ref, s_ref):
    return jnp.where(pl.program_id(0) < npb, p_ref[...], s_ref[...])


def _gates_kernel(hp_ref, hs_ref, w_ref, o_ref, *, tn, npb):
    h = _pair_rows(npb, hp_ref, hs_ref)
    for j in range(w_ref.shape[0] // tn):
        cols = slice(j * tn, (j + 1) * tn)
        o_ref[:, cols] = _sigmoid(_dot_nt(h, w_ref[cols, :])).astype(BF16)


def _gates(h, wt):
    d = h[0].shape[1]
    nc = wt.shape[0]
    tm, npb, nblk = _pair_tile(h)
    tn = _pick_tile([nc], (1024, 512, 256, 128))
    return pl.pallas_call(
        functools.partial(_gates_kernel, tn=tn, npb=npb),
        grid=(nblk,),
        in_specs=_split_specs(tm, d, npb) + [_resident(wt.shape)],
        out_specs=pl.BlockSpec((tm, nc), lambda i: (i, 0)),
        out_shape=jax.ShapeDtypeStruct((nblk * tm, nc), BF16),
        compiler_params=pltpu.CompilerParams(
            dimension_semantics=("parallel",), vmem_limit_bytes=VMEM_LIMIT),
        name="gates",
    )(h[0], h[1], wt)


def _hgrn_proj_kernel(hp_ref, hs_ref, w_ref, lbl_ref, hv_ref, lf_ref, *, layer, c, npb):
    h = _pair_rows(npb, hp_ref, hs_ref)

    def proj(j):
        return _dot_nt(h, w_ref[j * c:(j + 1) * c, :])

    for j in range(2):
        hv_ref[:, j * c:(j + 1) * c] = proj(j).astype(BF16)
    y = proj(2)
    hv_ref[:, 2 * c:3 * c] = (y * _sigmoid(y)).astype(BF16)
    y = proj(3)
    logits = lbl_ref[...]
    e = jnp.exp(logits - jnp.max(logits, axis=0, keepdims=True))
    lb = jnp.sum(e[:layer + 1], axis=0, keepdims=True) / jnp.sum(e, axis=0, keepdims=True)
    lf_ref[...] = jnp.log(lb + (1.0 - lb) * _sigmoid(y))
    hv_ref[:, 3 * c:4 * c] = ((1.0 - lb) * _sigmoid(-y)).astype(BF16)


def _hgrn_proj(h, wt, lb_logits, layer):
    d = h[0].shape[1]
    c = wt.shape[0] // 4
    tm, npb, nblk = _pair_tile(h)
    n = nblk * tm
    return pl.pallas_call(
        functools.partial(_hgrn_proj_kernel, layer=layer, c=c, npb=npb),
        grid=(nblk,),
        in_specs=_split_specs(tm, d, npb) + [_resident(wt.shape), _resident(lb_logits.shape)],
        out_specs=[pl.BlockSpec((tm, 4 * c), lambda i: (i, 0)),
                   pl.BlockSpec((tm, c), lambda i: (i, 0))],
        out_shape=[jax.ShapeDtypeStruct((n, 4 * c), BF16),
                   jax.ShapeDtypeStruct((n, c), F32)],
        compiler_params=pltpu.CompilerParams(
            dimension_semantics=("parallel",), vmem_limit_bytes=VMEM_LIMIT),
        name="hgrn_proj",
    )(h[0], h[1], wt, lb_logits)


def _mla_prep_kernel(hp_ref, hs_ref, wm_ref, qn_ref, kvn_ref, wuq_ref, wuk_ref, cs_ref,
                     qc_ref, kc_ref, ckvp_ref, ckvs_ref, kpep_ref, kpes_ref,
                     *, heads, q_lora, kv_lora, nope, rope, scale, npb):
    i = pl.program_id(0)
    tm = hp_ref.shape[0]
    c = _dot_nt(_pair_rows(npb, hp_ref, hs_ref), wm_ref[...])
    cos = cs_ref[:, :LANES]
    sin = cs_ref[:, LANES:]
    ckv = _rms(c[:, q_lora:q_lora + kv_lora], kvn_ref[...])
    o = q_lora + kv_lora
    kpe = c[:, o:o + LANES] * cos + c[:, o + LANES:o + 2 * LANES] * sin
    kc_ref[:, :kv_lora] = ckv.astype(BF16)
    kc_ref[:, kv_lora:] = kpe.astype(BF16)

    @pl.when(i < npb)
    def _():
        ckvp_ref[...] = ckv
        kpep_ref[...] = kpe[:, :rope]

    @pl.when(i >= npb)
    def _():
        ckvs_ref[...] = ckv
        kpes_ref[...] = kpe[:, :rope]

    qn = _rms(c[:, :q_lora], qn_ref[...]).astype(BF16)
    q3 = _dot(qn, wuq_ref[...]) * scale
    nb = tm // LANES
    for h in range(heads):
        qh = q3[:, h * nope:(h + 1) * nope].astype(BF16)
        lat = _dot(qh, wuk_ref[h]).astype(BF16)
        qc_ref[:, h, :, :kv_lora] = lat.reshape(nb, LANES, kv_lora)
        r0 = heads * nope + h * LANES
        r1 = heads * nope + heads * LANES + h * LANES
        pe = (q3[:, r0:r0 + LANES] * cos + q3[:, r1:r1 + LANES] * sin).astype(BF16)
        qc_ref[:, h, :, kv_lora:] = pe.reshape(nb, LANES, LANES)


def _mla_prep(h, w_mlat, q_norm, kv_norm, w_uq3, w_ukt, cs, *, heads, q_lora, kv_lora, nope, rope, scale):
    d = h[0].shape[1]
    ptok, stok = h[0].shape[0], h[1].shape[0]
    tm, npb, nblk = _pair_tile(h)
    n = nblk * tm
    nb = tm // LANES
    cw = kv_lora + LANES
    kern = functools.partial(_mla_prep_kernel, heads=heads, q_lora=q_lora, kv_lora=kv_lora,
                             nope=nope, rope=rope, scale=scale, npb=npb)
    return pl.pallas_call(
        kern,
        grid=(nblk,),
        in_specs=_split_specs(tm, d, npb)
                 + [_resident(w_mlat.shape), _resident(q_norm.shape), _resident(kv_norm.shape),
                    _resident(w_uq3.shape), _resident(w_ukt.shape),
                    pl.BlockSpec((tm, 2 * LANES), lambda i: (i, 0))],
        out_specs=[pl.BlockSpec((nb, heads, LANES, cw), lambda i: (i, 0, 0, 0)),
                   pl.BlockSpec((tm, cw), lambda i: (i, 0))]
                  + _split_specs(tm, kv_lora, npb) + _split_specs(tm, rope, npb),
        out_shape=[jax.ShapeDtypeStruct((n // LANES, heads, LANES, cw), BF16),
                   jax.ShapeDtypeStruct((n, cw), BF16),
                   jax.ShapeDtypeStruct((ptok, kv_lora), F32),
                   jax.ShapeDtypeStruct((stok, kv_lora), F32),
                   jax.ShapeDtypeStruct((ptok, rope), F32),
                   jax.ShapeDtypeStruct((stok, rope), F32)],
        compiler_params=pltpu.CompilerParams(
            dimension_semantics=("arbitrary",), vmem_limit_bytes=VMEM_LIMIT),
        name="mla_prep",
    )(h[0], h[1], w_mlat, q_norm, kv_norm, w_uq3, w_ukt, cs)


def _attn_prompt_kernel(q_ref, k_ref, wuv_ref, o_ref, m_scr, l_scr, acc_scr,
                        *, tq, tk, heads, vdim, c, nchunk):
    i = pl.program_id(1)
    hpc = heads // nchunk
    rc = hpc * tq
    m_scr[...] = jnp.full(m_scr.shape, -jnp.inf, F32)
    l_scr[...] = jnp.zeros(l_scr.shape, F32)
    acc_scr[...] = jnp.zeros(acc_scr.shape, F32)

    assert tq == LANES and tk % tq == 0

    def step(off, width, masked):
        nt = width // LANES
        k = k_ref[pl.ds(off, width), :]
        v = k[:, :c]
        if masked:
            t_row = lax.broadcasted_iota(jnp.int32, (rc, 1), 0) & (tq - 1)
            keep = lax.broadcasted_iota(jnp.int32, (1, tq), 1) <= t_row

        def scores(r):
            q = q_ref[0, r * hpc:(r + 1) * hpc].reshape(rc, q_ref.shape[-1])
            s = _dot_nt(q, k)
            if masked:
                last = jnp.where(keep, s[:, width - tq:], -jnp.inf)
                s = last if nt == 1 else jnp.concatenate([s[:, :width - tq], last], axis=1)
            return s

        def update(r, s):
            rs = slice(r * rc, (r + 1) * rc)
            m_prev = m_scr[rs]
            smax = s[:, :LANES]
            for t in range(1, nt):
                smax = jnp.maximum(smax, s[:, t * LANES:(t + 1) * LANES])
            m_new = jnp.maximum(m_prev, jnp.max(smax, axis=-1, keepdims=True))
            alpha = jnp.exp2(m_prev - m_new)
            p = jnp.exp2(s - jnp.concatenate([m_new] * nt, axis=1))
            psum = p[:, :LANES]
            for t in range(1, nt):
                psum = psum + p[:, t * LANES:(t + 1) * LANES]
            l_scr[rs] = alpha * l_scr[rs] + psum
            acc_scr[rs] = jnp.concatenate([alpha] * (c // LANES), axis=1) * acc_scr[rs] + _dot(p.astype(BF16), v)
            m_scr[rs] = m_new

        s_cur = scores(0)
        for r in range(nchunk):
            s_next = scores(r + 1) if r + 1 < nchunk else None
            update(r, s_cur)
            s_cur = s_next

    ratio = tk // tq
    nfull = i // ratio

    def body(j, carry):
        step(pl.multiple_of(j * tk, tk), tk, False)
        return carry

    lax.fori_loop(0, nfull, body, 0)
    for rem in range(ratio):
        @pl.when(i % ratio == rem)
        def _():
            step(pl.multiple_of(nfull * tk, tk), (rem + 1) * tq, True)

    inv = 1.0 / jnp.sum(l_scr[...], axis=-1, keepdims=True)
    for h in range(heads):
        oh = (acc_scr[h * tq:(h + 1) * tq, :] * inv[h * tq:(h + 1) * tq]).astype(BF16)
        o_ref[:, h * vdim:(h + 1) * vdim] = _dot(oh, wuv_ref[:, h * vdim:(h + 1) * vdim]).astype(BF16)


def _attn_prompt(qc, kc, wuv, bsz, seq, *, heads, vdim, c):
    cw = kc.shape[1]
    tq = ATTN_TQ
    tk = min(ATTN_TK, seq)
    nq = seq // tq
    rows = heads * tq
    nchunk = ATTN_ROW_CHUNKS
    kern = functools.partial(_attn_prompt_kernel, tq=tq, tk=tk, heads=heads, vdim=vdim, c=c, nchunk=nchunk)
    return pl.pallas_call(
        kern,
        grid=(bsz, nq),
        in_specs=[pl.BlockSpec((1, heads, tq, cw), lambda bi, i: (bi * nq + i, 0, 0, 0)),
                  pl.BlockSpec((seq, cw), lambda bi, i: (bi, 0)),
                  _resident(wuv.shape)],
        out_specs=pl.BlockSpec((tq, heads * vdim), lambda bi, i: (bi * nq + i, 0)),
        out_shape=jax.ShapeDtypeStruct((bsz * seq, heads * vdim), BF16),
        scratch_shapes=[pltpu.VMEM((rows, LANES), F32), pltpu.VMEM((rows, LANES), F32),
                        pltpu.VMEM((rows, c), F32)],
        compiler_params=pltpu.CompilerParams(
            dimension_semantics=("parallel", "arbitrary"), vmem_limit_bytes=VMEM_LIMIT),
        name="attn_prompt",
    )(qc, kc, wuv)


def _attn_sample_kernel(pt_ref, q_ref, kn_ref, wuv_ref, cck_ref, cpe_ref, o_ref,
                        bufk, bufp, sem, m_scr, l_scr, acc_scr,
                        *, nseq, nsq, npages, ch, nbuf, heads, vdim, ntok, c, rope):
    cps = npages // ch
    total = (nseq // nsq) * cps

    def unit_copies(unit, slot):
        pair = unit // cps
        part = unit % cps
        copies = []
        for sq in range(nsq):
            base = (nsq * pair + sq) * npages + part * ch
            for pg in range(ch):
                page = pt_ref[base + pg]
                copies.append(pltpu.make_async_copy(
                    cck_ref.at[page], bufk.at[slot, sq, pl.ds(pg * PAGE_SIZE, PAGE_SIZE), :],
                    sem.at[0, slot]))
                copies.append(pltpu.make_async_copy(
                    cpe_ref.at[page], bufp.at[slot, sq, :, pl.ds(pg * PAGE_SIZE, PAGE_SIZE)],
                    sem.at[1, slot]))
        return copies

    def start_unit(unit):
        for cp in unit_copies(unit, unit % nbuf):
            cp.start()

    def wait_unit(unit):
        for cp in unit_copies(unit, unit % nbuf):
            cp.wait()

    for u0 in range(min(nbuf - 1, total)):
        start_unit(u0)

    def softmax_update(sq, s, v):
        m_prev = m_scr[sq]
        m_new = jnp.maximum(m_prev, jnp.max(s, axis=-1, keepdims=True))
        alpha = jnp.exp2(m_prev - m_new)
        p = jnp.exp2(s - m_new)
        l_scr[sq] = alpha * l_scr[sq] + jnp.sum(p, axis=-1, keepdims=True)
        acc_scr[sq] = alpha * acc_scr[sq] + _dot(p.astype(BF16), v)
        m_scr[sq] = m_new

    def body(u, carry):
        pair = u // cps
        part = u % cps
        slot = u % nbuf

        @pl.when(u + nbuf - 1 < total)
        def _():
            start_unit(u + nbuf - 1)

        wait_unit(u)

        @pl.when(part == 0)
        def _():
            m_scr[...] = jnp.full(m_scr.shape, -jnp.inf, F32)
            l_scr[...] = jnp.zeros(l_scr.shape, F32)
            acc_scr[...] = jnp.zeros(acc_scr.shape, F32)

        kbs, ss = [], []
        for sq in range(nsq):
            q = q_ref[nsq * pair + sq]
            kb = bufk[slot, sq].astype(BF16)
            kpt = bufp[slot, sq].astype(BF16)
            kbs.append(kb)
            ss.append(_dot_nt(q[:, :c], kb) + _dot(q[:, c:c + rope], kpt))
        for sq in range(nsq):
            softmax_update(sq, ss[sq], kbs[sq])

        @pl.when(part == cps - 1)
        def _():
            outs = []
            for sq in range(nsq):
                seq = nsq * pair + sq
                q = q_ref[seq]
                kn = kn_ref[seq]
                sn = _dot_nt(q, kn)
                rows, cols = sn.shape
                t_row = _div_pow2(lax.broadcasted_iota(jnp.int32, (rows, 1), 0), heads)
                col = lax.broadcasted_iota(jnp.int32, (1, cols), 1)
                sn = jnp.where((col <= t_row) & (col < ntok), sn, -jnp.inf)
                softmax_update(sq, sn, kn[:, :c])
                outs.append((acc_scr[sq] / l_scr[sq]).astype(BF16))
            y = _dot(jnp.concatenate(outs, axis=0), wuv_ref[...])
            h_row = lax.broadcasted_iota(jnp.int32, (nsq * rows, 1), 0) & (heads - 1)
            h_col = _div_pow2(lax.broadcasted_iota(jnp.int32, (1, heads * vdim), 1), vdim)
            z = jnp.where(h_row == h_col, y, 0.0)
            tid = lax.broadcasted_iota(jnp.int32, (ntok, 1), 0)
            for sq in range(nsq):
                out = jnp.zeros((ntok, heads * vdim), F32)
                for t in range(ntok):
                    r0 = sq * rows + t * heads
                    out = jnp.where(tid == t, jnp.sum(z[r0:r0 + heads], axis=0, keepdims=True), out)
                o_ref[nsq * pair + sq] = out

        return carry

    lax.fori_loop(0, total, body, 0)


def _attn_sample(page_table, qs, kn, wuv, cache_ckv, cache_kpet, *, heads, vdim, ntok, c):
    nseq, rows, _ = qs.shape
    npages = page_table.shape[1]
    rope = cache_kpet.shape[1]
    ch = min(SAMPLE_PAGES, npages)
    nsq = SAMPLE_SEQS
    assert npages % ch == 0 and nseq % nsq == 0
    nbuf = SAMPLE_BUFS
    kern = functools.partial(_attn_sample_kernel, nseq=nseq, nsq=nsq, npages=npages, ch=ch, nbuf=nbuf,
                             heads=heads, vdim=vdim, ntok=ntok, c=c, rope=rope)
    vmem = pl.BlockSpec(memory_space=pltpu.VMEM)
    return pl.pallas_call(
        kern,
        in_specs=[pl.BlockSpec(memory_space=pltpu.SMEM), vmem, vmem, vmem,
                  pl.BlockSpec(memory_space=pl.ANY), pl.BlockSpec(memory_space=pl.ANY)],
        out_specs=vmem,
        out_shape=jax.ShapeDtypeStruct((nseq, ntok, heads * vdim), F32),
        scratch_shapes=[pltpu.VMEM((nbuf, nsq, ch * PAGE_SIZE, c), F32),
                        pltpu.VMEM((nbuf, nsq, rope, ch * PAGE_SIZE), F32),
                        pltpu.SemaphoreType.DMA((2, nbuf)),
                        pltpu.VMEM((nsq, rows, 1), F32), pltpu.VMEM((nsq, rows, 1), F32),
                        pltpu.VMEM((nsq, rows, c), F32)],
        compiler_params=pltpu.CompilerParams(vmem_limit_bytes=VMEM_LIMIT),
        name="attn_sample",
    )(page_table.reshape(-1), qs, kn, wuv, cache_ckv, cache_kpet)


def _hgrn_kernel(hv_ref, lf_ref, s0_ref, gn_ref, ob_ref, so_ref,
                 s_scr, q_scr, k_scr, b_scr, at_scr,
                 *, L, G, sub, n_t, heads, dk, dv):
    c = pl.program_id(1)
    nc = pl.num_programs(1)

    @pl.when(c == 0)
    def _():
        s_scr[...] = s0_ref[...]

    for gi in range(G):
        _hgrn_chunk(hv_ref, lf_ref, gn_ref, ob_ref, s_scr, q_scr, k_scr, b_scr, at_scr,
                    gi, L=L, sub=sub, n_t=n_t, heads=heads, dk=dk, dv=dv)

    @pl.when(c == nc - 1)
    def _():
        so_ref[...] = s_scr[...]


def _hgrn_chunk(hv_ref, lf_ref, gn_ref, ob_ref, s_scr, q_scr, k_scr, b_scr, at_scr,
                gi, *, L, sub, n_t, heads, dk, dv):
    hd = heads * dk
    nsub = L // sub
    levels = int(math.log2(nsub))
    assert 2 ** levels == nsub
    rows = slice(gi * L, (gi + 1) * L)

    g = lf_ref[rows, :]
    g1 = g.astype(BF16)
    r1 = g - g1.astype(F32)
    g2 = r1.astype(BF16)
    g3 = (r1 - g2.astype(F32)).astype(BF16)
    row = lax.broadcasted_iota(jnp.int32, (L, L), 0)
    col = lax.broadcasted_iota(jnp.int32, (L, L), 1)
    tril = jnp.where(col <= row, 1.0, 0.0).astype(BF16)
    b = _dot(tril, g1) + _dot(tril, g2) + _dot(tril, g3)

    def group_bounds(grp):
        ends = [b[(j + 1) * grp - 1:(j + 1) * grp, :] for j in range(L // grp)]
        starts = [jnp.zeros_like(ends[0])] + ends[:-1]
        rep = lambda rows_: jnp.concatenate([jnp.broadcast_to(r, (grp, hd)) for r in rows_], axis=0)
        return rep(starts), rep(ends)

    bstart, bend = group_bounds(sub)
    brel = b - bstart
    blast = b[L - 1:L, :]

    q = hv_ref[rows, 0:hd].astype(F32)
    v = hv_ref[rows, hd:2 * hd]
    sg = hv_ref[rows, 2 * hd:3 * hd].astype(F32)
    k = hv_ref[rows, 3 * hd:4 * hd].astype(F32)

    q_scr[rows, :] = q
    k_scr[rows, :] = k
    b_scr[rows, :] = brel * math.log2(math.e)

    qt = q * jnp.exp(brel)
    kh = k * jnp.exp(bend - b)
    q_in = (qt * jnp.exp(bstart)).astype(BF16)
    k_st = (kh * jnp.exp(blast - bend)).astype(BF16)
    e_last = jnp.exp(blast)

    lhs = [qt.astype(BF16)]
    rhs = [kh.astype(BF16)]
    valid = []
    for lv in range(levels):
        grp = sub * (2 ** lv)
        if lv > 0:
            gs, ge = group_bounds(grp)
            lhs.append((qt * jnp.exp(bstart - gs)).astype(BF16))
            rhs.append((kh * jnp.exp(ge - bend)).astype(BF16))
        cg = _div_pow2(col, grp)
        valid.append(((cg & 1) == 1) & (_div_pow2(row, grp) == cg - 1))

    rid8 = lax.broadcasted_iota(jnp.int32, (8, 1), 0)
    lane = lax.broadcasted_iota(jnp.int32, (8, dk), 1)
    lss = [slice(h * dk, (h + 1) * dk) for h in range(heads)]
    halves = sub // 8

    def diag_body(i, carry):
        r0 = pl.multiple_of(gi * L + i * sub, sub)
        at = [[jnp.zeros((8, dk), F32) for _ in range(halves)] for _ in range(heads)]
        blk = [[ref[pl.ds(r0, sub), ls] for ref in (q_scr, k_scr, b_scr)] for ls in lss]
        for t in range(n_t):
            is_col = lane == i * sub + t
            for h in range(heads):
                qb, kb, bb = blk[h]
                for hf in range(t // 8 + 1):
                    rr = slice(8 * hf, 8 * hf + 8)
                    w = jnp.exp2(bb[t:t + 1, :] - bb[rr]) * kb[rr] * qb[t:t + 1, :]
                    a = jnp.sum(w, axis=-1, keepdims=True)
                    if t < 8 * hf + 7:
                        a = jnp.where(rid8 + 8 * hf <= t, a, 0.0)
                    at[h][hf] = jnp.where(is_col, a, at[h][hf])
        for h, ls in enumerate(lss):
            for hf in range(halves):
                at_scr[pl.ds(pl.multiple_of(r0 + 8 * hf, 8), 8), ls] = at[h][hf]
        return carry

    lax.fori_loop(0, nsub, diag_body, 0)

    gn = gn_ref[...]
    s_prev = [s_scr[gi, h] for h in range(heads)]
    o_in = [_dot(q_in[:, ls], s_prev[h].astype(BF16)) for h, ls in enumerate(lss)]
    p_off = [[_dot_nt(rhs[lv][:, ls], lhs[lv][:, ls]) for lv in range(levels)] for ls in lss]
    s_upd = [_dot_tn(k_st[:, ls], v[:, ls]) for ls in lss]
    o_mix = []
    for h, ls in enumerate(lss):
        if levels > 0:
            a_t = at_scr[rows, h * dk:h * dk + L]
            for lv in range(levels):
                a_t = a_t + jnp.where(valid[lv], p_off[h][lv], 0.0)
            o_mix.append(_dot_tn(a_t.astype(BF16), v[:, ls]))
        else:
            o_mix.append(_dot_tn(at_scr[rows, ls].astype(BF16), v[:, ls])[:L])
    for h, ls in enumerate(lss):
        decay = jnp.transpose(jnp.broadcast_to(e_last[:, ls], (dk, dk)))
        s_scr[gi, h] = decay * s_prev[h] + s_upd[h]
        o = o_in[h] + o_mix[h]
        on = o * lax.rsqrt(jnp.mean(o * o, axis=-1, keepdims=True) + EPS) * gn
        ob_ref[rows, ls] = (on * sg[:, ls]).astype(BF16)


def _hgrn(hv, lf, s0, gn, nc, *, L, G, sub, n_t, heads, dk, dv):
    nseq = s0.shape[0]
    rows = nseq * nc * L
    hd = heads * dk
    assert nseq % G == 0 and (G == 1 or nc == 1)
    kern = functools.partial(_hgrn_kernel, L=L, G=G, sub=sub, n_t=n_t, heads=heads, dk=dk, dv=dv)
    return pl.pallas_call(
        kern,
        grid=(nseq // G, nc),
        in_specs=[pl.BlockSpec((G * L, 4 * hd), lambda s, c: (s * nc + c, 0)),
                  pl.BlockSpec((G * L, hd), lambda s, c: (s * nc + c, 0)),
                  pl.BlockSpec((G, heads, dk, dv), lambda s, c: (s, 0, 0, 0)),
                  pl.BlockSpec((1, dv), lambda s, c: (0, 0))],
        out_specs=[pl.BlockSpec((G * L, heads * dv), lambda s, c: (s * nc + c, 0)),
                   pl.BlockSpec((G, heads, dk, dv), lambda s, c: (s, 0, 0, 0))],
        out_shape=[jax.ShapeDtypeStruct((rows, heads * dv), BF16),
                   jax.ShapeDtypeStruct((nseq, heads, dk, dv), F32)],
        scratch_shapes=[pltpu.VMEM((G, heads, dk, dv), F32)] + [pltpu.VMEM((G * L, hd), F32)] * 4,
        compiler_params=pltpu.CompilerParams(
            dimension_semantics=("parallel", "arbitrary"), vmem_limit_bytes=VMEM_LIMIT),
        name="hgrn",
    )(hv, lf, s0, gn)


def _merge_kernel(oap_ref, oas_ref, obp_ref, obs_ref, xp_ref, xs_ref, sg_ref, wa_ref, wb_ref, wo_ref, o_ref,
                  *, d, npb):
    a = _dot(_pair_rows(npb, oap_ref, oas_ref), wa_ref[...])
    b = _dot(_pair_rows(npb, obp_ref, obs_ref), wb_ref[...])
    merged = sg_ref[:, :d].astype(F32) * a + sg_ref[:, d:].astype(F32) * b
    o_ref[...] = _pair_rows(npb, xp_ref, xs_ref) + _dot(merged.astype(BF16), wo_ref[...])


def _merge(oa, ob, x, sg, wa, wb, wo):
    d = x[0].shape[1]
    tm, npb, nblk = _pair_tile(x, (256, 128))
    return pl.pallas_call(
        functools.partial(_merge_kernel, d=d, npb=npb),
        grid=(nblk,),
        in_specs=_split_specs(tm, oa[0].shape[1], npb) + _split_specs(tm, ob[0].shape[1], npb)
                 + _split_specs(tm, d, npb)
                 + [pl.BlockSpec((tm, 2 * d), lambda i: (i, 0)),
                    _resident(wa.shape), _resident(wb.shape), _resident(wo.shape)],
        out_specs=pl.BlockSpec((tm, d), lambda i: (i, 0)),
        out_shape=jax.ShapeDtypeStruct((nblk * tm, d), F32),
        compiler_params=pltpu.CompilerParams(
            dimension_semantics=("parallel",), vmem_limit_bytes=VMEM_LIMIT),
        name="merge_out",
    )(oa[0], oa[1], ob[0], ob[1], x[0], x[1], sg, wa, wb, wo)


def _ple_kernel(x_ref, pp_ref, ps_ref, g_ref, wg_ref, wp_ref, gf_ref, yp_ref, ys_ref, *, final, npb):
    i = pl.program_id(0)
    x = x_ref[...]
    gate = _sigmoid(_dot(_rms(x, g_ref[...]).astype(BF16), wg_ref[...]))
    p = jnp.where(i < npb, pp_ref[...], ps_ref[...])
    y = x + gate * _dot(p.astype(BF16), wp_ref[...])
    if final:
        y = _rms(y, gf_ref[...])

    @pl.when(i < npb)
    def _():
        yp_ref[...] = y

    @pl.when(i >= npb)
    def _():
        ys_ref[...] = y


def _ple(x, pp, ps, g, wg, wp, gf, final):
    n, d = x.shape
    ptok, stok = pp.shape[0], ps.shape[0]
    tm = _pick_tile([ptok, stok], (512, 256, 128))
    npb = ptok // tm
    return pl.pallas_call(
        functools.partial(_ple_kernel, final=final, npb=npb),
        grid=(n // tm,),
        in_specs=[pl.BlockSpec((tm, d), lambda i: (i, 0))] + _split_specs(tm, pp.shape[1], npb)
                 + [_resident(g.shape), _resident(wg.shape), _resident(wp.shape), _resident(gf.shape)],
        out_specs=_split_specs(tm, d, npb),
        out_shape=[jax.ShapeDtypeStruct((ptok, d), F32), jax.ShapeDtypeStruct((stok, d), F32)],
        compiler_params=pltpu.CompilerParams(
            dimension_semantics=("arbitrary",), vmem_limit_bytes=VMEM_LIMIT),
        name="ple",
    )(x, pp, ps, g, wg, wp, gf)


def _rot_last(w):
    half = w.shape[-1] // 2
    return jnp.concatenate([-w[..., half:], w[..., :half]], axis=-1)


def _pad_to_lanes(w, axis):
    pads = [(0, 0)] * w.ndim
    pads[axis] = (0, LANES - w.shape[axis])
    return jnp.pad(w, pads)


def kernel(x_prompt, x_sample, cache_ckv, cache_kpe, state_hgrn, page_table, p_prompt, p_sample, ffn1_norm, ffn1_w_gate, ffn1_w_up, ffn1_w_down, mix_norm, w_in, q_norm, w_uq, kv_norm, w_uk, w_uv, hgrn_lb_logits, hgrn_out_norm, w_branch_a, w_branch_b, w_out, ffn2_norm, ffn2_w_gate, ffn2_w_up, ffn2_w_down, ple_norm, w_ple_gate, w_ple_proj, final_norm):
    bsz, seq, d = x_prompt.shape
    nseq, ntok, _ = x_sample.shape
    depth = w_in.shape[0]
    q_lora = q_norm.shape[1]
    kv_lora, heads, nope = w_uk.shape[1:]
    vdim = w_uv.shape[-1]
    rope = cache_kpe.shape[-1]
    hb, dk, dv = state_hgrn.shape[2:]
    hd = hb * dk
    npages = page_table.shape[1]
    past_len = npages * PAGE_SIZE
    ptok = bsz * seq
    stok = nseq * ntok
    scale = math.log2(math.e) / math.sqrt(nope + rope)
    assert rope <= LANES and nope == LANES and dk == LANES and dv == LANES

    x = (x_prompt.reshape(ptok, d), x_sample.reshape(stok, d))

    half = rope // 2
    inv = ROPE_THETA ** (-jnp.arange(half, dtype=F32) / half)
    pos = jnp.concatenate([jnp.tile(jnp.arange(seq, dtype=F32), bsz),
                           jnp.tile(jnp.arange(ntok, dtype=F32) + past_len, nseq)])
    ang = pos[:, None] * inv[None, :]
    cos = jnp.cos(ang)
    sin = jnp.sin(ang)
    cs = jnp.concatenate([_pad_to_lanes(jnp.concatenate([cos, cos], axis=1), 1),
                          _pad_to_lanes(jnp.concatenate([sin, sin], axis=1), 1)], axis=1)

    sp = [0]
    for w in (q_lora, kv_lora, rope, hd, hd, hb * dv, hb * dv, d, d):
        sp.append(sp[-1] + w)

    ckv_p, kpe_p, st_p, ckv_s, kpe_s, st_s = [], [], [], [], [], []
    for i in range(depth):
        wit = jnp.swapaxes(w_in[i], 0, 1)
        seg = [wit[sp[j]:sp[j + 1]] for j in range(9)]
        kr_t = seg[2]
        kr_rot_t = jnp.swapaxes(_rot_last(jnp.swapaxes(kr_t, 0, 1)), 0, 1)
        w_mlat = jnp.concatenate([seg[0], seg[1], _pad_to_lanes(kr_t, 0), _pad_to_lanes(kr_rot_t, 0)],
                                 axis=0).astype(BF16)
        w_hgt = jnp.concatenate([seg[3], seg[5], seg[6], seg[4]], axis=0).astype(BF16)
        w_gtt = jnp.concatenate([seg[7], seg[8]], axis=0).astype(BF16)
        wq = w_uq[i].reshape(q_lora, heads, nope + rope)
        wq_rope = wq[:, :, nope:]
        w_uq3 = jnp.concatenate([wq[:, :, :nope].reshape(q_lora, heads * nope),
                                 _pad_to_lanes(wq_rope, 2).reshape(q_lora, heads * LANES),
                                 _pad_to_lanes(_rot_last(wq_rope), 2).reshape(q_lora, heads * LANES)],
                                axis=1).astype(BF16)
        w_ukt = jnp.transpose(w_uk[i], (1, 2, 0)).astype(BF16)
        w_uv2 = w_uv[i].reshape(kv_lora, heads * vdim).astype(BF16)

        ffn1 = (ffn1_norm[i][None], ffn1_w_gate[i].astype(BF16), ffn1_w_up[i].astype(BF16),
                ffn1_w_down[i].astype(BF16), mix_norm[i][None])
        (x_p, hmix_p), (x_s, hmix_s) = _ffn(x[0], *ffn1), _ffn(x[1], *ffn1)
        x, hmix = (x_p, x_s), (hmix_p, hmix_s)

        sg = _gates(hmix, w_gtt)
        hv, lf = _hgrn_proj(hmix, w_hgt, hgrn_lb_logits, i)
        qc, kc, ckv_fp, ckv_fs, kpe_fp, kpe_fs = _mla_prep(
            hmix, w_mlat, q_norm[i][None], kv_norm[i][None], w_uq3, w_ukt, cs,
            heads=heads, q_lora=q_lora, kv_lora=kv_lora, nope=nope, rope=rope, scale=scale)

        npb = ptok // LANES
        oa_p = _attn_prompt(qc, kc, w_uv2, bsz, seq, heads=heads, vdim=vdim, c=kv_lora)
        cw = qc.shape[-1]
        qs = jnp.transpose(qc[npb:], (0, 2, 1, 3)).reshape(nseq, ntok * heads, cw)
        kn = jnp.pad(kc[ptok:].reshape(nseq, ntok, cw), ((0, 0), (0, 16 - ntok), (0, 0)))
        oa_s = _attn_sample(page_table, qs, kn, w_uv2, cache_ckv[i], jnp.swapaxes(cache_kpe[i], 1, 2),
                            heads=heads, vdim=vdim, ntok=ntok, c=kv_lora)
        oa = (oa_p, oa_s.reshape(stok, heads * vdim).astype(BF16))

        gn = hgrn_out_norm[i][None]
        ob_p, s_p = _hgrn(hv, lf, jnp.zeros((bsz, hb, dk, dv), F32), gn, seq // HGRN_CHUNK,
                          L=HGRN_CHUNK, G=1, sub=HGRN_SUB, n_t=HGRN_SUB, heads=hb, dk=dk, dv=dv)
        srows = HGRN_SAMPLE_ROWS
        pad = srows - ntok
        hv_s = jnp.pad(hv[ptok:].reshape(nseq, ntok, 4 * hd), ((0, 0), (0, pad), (0, 0)))
        lf_s = jnp.pad(lf[ptok:].reshape(nseq, ntok, hd), ((0, 0), (0, pad), (0, 0)))
        ob_s, s_s = _hgrn(hv_s.reshape(nseq * srows, 4 * hd), lf_s.reshape(nseq * srows, hd),
                          state_hgrn[i], gn, 1, L=srows, G=HGRN_SAMPLE_GROUP, sub=srows, n_t=ntok,
                          heads=hb, dk=dk, dv=dv)
        ob = (ob_p, ob_s.reshape(nseq, srows, hb * dv)[:, :ntok].reshape(stok, hb * dv))

        x = _merge(oa, ob, x, sg, w_branch_a[i].astype(BF16), w_branch_b[i].astype(BF16),
                   w_out[i].astype(BF16))
        x = _ffn(x, ffn2_norm[i][None], ffn2_w_gate[i].astype(BF16), ffn2_w_up[i].astype(BF16),
                 ffn2_w_down[i].astype(BF16))
        x = tuple(_ple(x, p_prompt[i].reshape(ptok, -1), p_sample[i].reshape(stok, -1), ple_norm[i][None],
                       w_ple_gate[i].astype(BF16), w_ple_proj[i].astype(BF16), final_norm[None],
                       i == depth - 1))

        ckv_p.append(ckv_fp.reshape(bsz, seq, kv_lora))
        kpe_p.append(kpe_fp.reshape(bsz, seq, rope))
        st_p.append(s_p)
        ckv_s.append(ckv_fs.reshape(nseq, ntok, kv_lora))
        kpe_s.append(kpe_fs.reshape(nseq, ntok, rope))
        st_s.append(s_s)

    return (x[0].reshape(bsz, seq, d), x[1].reshape(nseq, ntok, d),
            jnp.stack(ckv_p), jnp.stack(kpe_p), jnp.stack(st_p),
            jnp.stack(ckv_s), jnp.stack(kpe_s), jnp.stack(st_s))
```

```python
import functools
import math

import jax
import jax.numpy as jnp
from jax import lax
from jax.experimental import pallas as pl
from jax.experimental.pallas import tpu as pltpu

F32 = jnp.float32
BF16 = jnp.bfloat16

EPS = 1e-6
ROPE_THETA = 10000.0
PAGE_SIZE = 128
LANES = 128
HGRN_SUB = 16
HGRN_CHUNK = 64
HGRN_SAMPLE_ROWS = 16
HGRN_SAMPLE_GROUP = 8
ATTN_TQ = 128
ATTN_TK = 1024
ATTN_ROW_CHUNKS = 4
SAMPLE_SEQS = 4
SAMPLE_PAGES = 4
SAMPLE_BUFS = 4
VMEM_LIMIT = 60 * 1024 * 1024


def _sigmoid(x):
    return 1.0 / (1.0 + jnp.exp(-x))


def _rms(x, g):
    return x * lax.rsqrt(jnp.mean(x * x, axis=-1, keepdims=True) + EPS) * g


def _dot(a, b):
    return jnp.dot(a, b, preferred_element_type=F32)


def _dot_nt(a, b):
    return lax.dot_general(a, b, (((1,), (1,)), ((), ())), preferred_element_type=F32)


def _dot_tn(a, b):
    return lax.dot_general(a, b, (((0,), (0,)), ((), ())), preferred_element_type=F32)


def _div_pow2(x, n):
    shift = int(math.log2(n))
    assert 2 ** shift == n
    return x >> shift


def _pick_tile(sizes, candidates):
    for c in candidates:
        if all(n % c == 0 for n in sizes):
            return c
    raise ValueError(f"no tile in {candidates} divides {sizes}")


def _resident(shape):
    nd = len(shape)
    return pl.BlockSpec(shape, lambda *_: (0,) * nd, pipeline_mode=pl.Buffered(1))


def _split_specs(tm, width, npb):
    return [pl.BlockSpec((tm, width), lambda i: (jnp.minimum(i, npb - 1), 0)),
            pl.BlockSpec((tm, width), lambda i: (jnp.maximum(i - npb, 0), 0))]


def _ffn_kernel(*refs, nf, post, cast):
    refs = list(refs)
    x_ref, g_ref, wg_ref, wu_ref, wd_ref = refs[:5]
    refs = refs[5:]
    g2_ref = refs.pop(0) if post else None
    o_ref = refs.pop(0)
    h2_ref = refs.pop(0) if post else None
    wb_refs = [refs.pop(0) for _ in range(3)] if cast else None
    h_scr, acc_scr = refs
    f = pl.program_id(1)

    @pl.when(f == 0)
    def _():
        h_scr[...] = _rms(x_ref[...], g_ref[...]).astype(BF16)
        acc_scr[...] = jnp.zeros(acc_scr.shape, F32)

    wg, wu, wd = wg_ref[...], wu_ref[...], wd_ref[...]
    if cast:
        wg, wu, wd = wg.astype(BF16), wu.astype(BF16), wd.astype(BF16)
        for ref, w in zip(wb_refs, (wg, wu, wd)):
            ref[...] = w
    h = h_scr[...]
    g = _dot(h, wg)
    u = _dot(h, wu)
    a = (g * _sigmoid(g) * u).astype(BF16)
    acc_scr[...] += _dot(a, wd)

    @pl.when(f == nf - 1)
    def _():
        y = x_ref[...] + 0.5 * acc_scr[...]
        o_ref[...] = y
        if post:
            h2_ref[...] = _rms(y, g2_ref[...]).astype(BF16)


def _ffn(x, g, wg, wu, wd, g2=None, *, row0=0, nrows=None):
    d = x.shape[1]
    n = x.shape[0] if nrows is None else nrows
    dff = wg.shape[1]
    cast = wg.dtype == F32
    tm = _pick_tile([n, row0] if row0 else [n], (512, 256, 128))
    tf = _pick_tile([dff], (256, 128) if cast else (512, 256, 128))
    nf = dff // tf
    blk0 = row0 // tm
    post = g2 is not None
    assert not cast or n == tm
    w_specs = [
        pl.BlockSpec((d, tf), lambda i, f: (0, f)),
        pl.BlockSpec((d, tf), lambda i, f: (0, f)),
        pl.BlockSpec((tf, d), lambda i, f: (f, 0)),
    ]
    in_specs = [pl.BlockSpec((tm, d), lambda i, f: (i + blk0, 0)),
                pl.BlockSpec((1, d), lambda i, f: (0, 0))] + w_specs
    args = [x, g, wg, wu, wd]
    out_shape = [jax.ShapeDtypeStruct((n, d), F32)]
    out_specs = [pl.BlockSpec((tm, d), lambda i, f: (i, 0))]
    if post:
        in_specs.append(pl.BlockSpec((1, d), lambda i, f: (0, 0)))
        args.append(g2)
        out_shape.append(jax.ShapeDtypeStruct((n, d), BF16))
        out_specs.append(pl.BlockSpec((tm, d), lambda i, f: (i, 0)))
    if cast:
        out_shape += [jax.ShapeDtypeStruct(w.shape, BF16) for w in (wg, wu, wd)]
        out_specs += w_specs
    res = pl.pallas_call(
        functools.partial(_ffn_kernel, nf=nf, post=post, cast=cast),
        grid=(n // tm, nf),
        in_specs=in_specs,
        out_specs=out_specs,
        out_shape=out_shape,
        scratch_shapes=[pltpu.VMEM((tm, d), BF16), pltpu.VMEM((tm, d), F32)],
        compiler_params=pltpu.CompilerParams(
            dimension_semantics=("parallel", "arbitrary"), vmem_limit_bytes=VMEM_LIMIT),
        name="ffn_post" if post else "ffn",
    )(*args)
    nres = 2 if post else 1
    acts = tuple(res[:nres]) if post else res[0]
    return (acts, tuple(res[nres:])) if cast else acts


def _pair_tile(pair, candidates=(512, 256, 128)):
    ptok, stok = pair[0].shape[0], pair[1].shape[0]
    tm = _pick_tile([ptok, stok], candidates)
    return tm, ptok // tm, (ptok + stok) // tm


def _pair_rows(npb, p_ref, s_ref):
    return jnp.where(pl.program_id(0) < npb, p_ref[...], s_ref[...])


def _gates_kernel(hp_ref, hs_ref, w_ref, o_ref, *, tn, npb):
    h = _pair_rows(npb, hp_ref, hs_ref)
    for j in range(w_ref.shape[0] // tn):
        cols = slice(j * tn, (j + 1) * tn)
        o_ref[:, cols] = _sigmoid(_dot_nt(h, w_ref[cols, :])).astype(BF16)


def _gates(h, wt):
    d = h[0].shape[1]
    nc = wt.shape[0]
    tm, npb, nblk = _pair_tile(h)
    tn = _pick_tile([nc], (1024, 512, 256, 128))
    return pl.pallas_call(
        functools.partial(_gates_kernel, tn=tn, npb=npb),
        grid=(nblk,),
        in_specs=_split_specs(tm, d, npb) + [_resident(wt.shape)],
        out_specs=pl.BlockSpec((tm, nc), lambda i: (i, 0)),
        out_shape=jax.ShapeDtypeStruct((nblk * tm, nc), BF16),
        compiler_params=pltpu.CompilerParams(
            dimension_semantics=("parallel",), vmem_limit_bytes=VMEM_LIMIT),
        name="gates",
    )(h[0], h[1], wt)


def _hgrn_proj_kernel(hp_ref, hs_ref, w_ref, lbl_ref, hv_ref, lf_ref, *, layer, c, npb):
    h = _pair_rows(npb, hp_ref, hs_ref)

    def proj(j):
        return _dot_nt(h, w_ref[j * c:(j + 1) * c, :])

    for j in range(2):
        hv_ref[:, j * c:(j + 1) * c] = proj(j).astype(BF16)
    y = proj(2)
    hv_ref[:, 2 * c:3 * c] = (y * _sigmoid(y)).astype(BF16)
    y = proj(3)
    logits = lbl_ref[...]
    e = jnp.exp(logits - jnp.max(logits, axis=0, keepdims=True))
    lb = jnp.sum(e[:layer + 1], axis=0, keepdims=True) / jnp.sum(e, axis=0, keepdims=True)
    lf_ref[...] = jnp.log(lb + (1.0 - lb) * _sigmoid(y))
    hv_ref[:, 3 * c:4 * c] = ((1.0 - lb) * _sigmoid(-y)).astype(BF16)


def _hgrn_proj(h, wt, lb_logits, layer):
    d = h[0].shape[1]
    c = wt.shape[0] // 4
    tm, npb, nblk = _pair_tile(h)
    n = nblk * tm
    return pl.pallas_call(
        functools.partial(_hgrn_proj_kernel, layer=layer, c=c, npb=npb),
        grid=(nblk,),
        in_specs=_split_specs(tm, d, npb) + [_resident(wt.shape), _resident(lb_logits.shape)],
        out_specs=[pl.BlockSpec((tm, 4 * c), lambda i: (i, 0)),
                   pl.BlockSpec((tm, c), lambda i: (i, 0))],
        out_shape=[jax.ShapeDtypeStruct((n, 4 * c), BF16),
                   jax.ShapeDtypeStruct((n, c), F32)],
        compiler_params=pltpu.CompilerParams(
            dimension_semantics=("parallel",), vmem_limit_bytes=VMEM_LIMIT),
        name="hgrn_proj",
    )(h[0], h[1], wt, lb_logits)


def _mla_prep_kernel(hp_ref, hs_ref, wm_ref, qn_ref, kvn_ref, wuq_ref, wuk_ref, cs_ref,
                     qc_ref, kc_ref, ckvp_ref, ckvs_ref, kpep_ref, kpes_ref,
                     *, heads, q_lora, kv_lora, nope, rope, scale, npb):
    i = pl.program_id(0)
    tm = hp_ref.shape[0]
    c = _dot_nt(_pair_rows(npb, hp_ref, hs_ref), wm_ref[...])
    cos = cs_ref[:, :LANES]
    sin = cs_ref[:, LANES:]
    ckv = _rms(c[:, q_lora:q_lora + kv_lora], kvn_ref[...])
    o = q_lora + kv_lora
    kpe = c[:, o:o + LANES] * cos + c[:, o + LANES:o + 2 * LANES] * sin
    kc_ref[:, :kv_lora] = ckv.astype(BF16)
    kc_ref[:, kv_lora:] = kpe.astype(BF16)

    @pl.when(i < npb)
    def _():
        ckvp_ref[...] = ckv
        kpep_ref[...] = kpe[:, :rope]

    @pl.when(i >= npb)
    def _():
        ckvs_ref[...] = ckv
        kpes_ref[...] = kpe[:, :rope]

    qn = _rms(c[:, :q_lora], qn_ref[...]).astype(BF16)
    q3 = _dot(qn, wuq_ref[...]) * scale
    nb = tm // LANES
    for h in range(heads):
        qh = q3[:, h * nope:(h + 1) * nope].astype(BF16)
        lat = _dot(qh, wuk_ref[h]).astype(BF16)
        qc_ref[:, h, :, :kv_lora] = lat.reshape(nb, LANES, kv_lora)
        r0 = heads * nope + h * LANES
        r1 = heads * nope + heads * LANES + h * LANES
        pe = (q3[:, r0:r0 + LANES] * cos + q3[:, r1:r1 + LANES] * sin).astype(BF16)
        qc_ref[:, h, :, kv_lora:] = pe.reshape(nb, LANES, LANES)


def _mla_prep(h, w_mlat, q_norm, kv_norm, w_uq3, w_ukt, cs, *, heads, q_lora, kv_lora, nope, rope, scale):
    d = h[0].shape[1]
    ptok, stok = h[0].shape[0], h[1].shape[0]
    tm, npb, nblk = _pair_tile(h)
    n = nblk * tm
    nb = tm // LANES
    cw = kv_lora + LANES
    kern = functools.partial(_mla_prep_kernel, heads=heads, q_lora=q_lora, kv_lora=kv_lora,
                             nope=nope, rope=rope, scale=scale, npb=npb)
    return pl.pallas_call(
        kern,
        grid=(nblk,),
        in_specs=_split_specs(tm, d, npb)
                 + [_resident(w_mlat.shape), _resident(q_norm.shape), _resident(kv_norm.shape),
                    _resident(w_uq3.shape), _resident(w_ukt.shape),
                    pl.BlockSpec((tm, 2 * LANES), lambda i: (i, 0))],
        out_specs=[pl.BlockSpec((nb, heads, LANES, cw), lambda i: (i, 0, 0, 0)),
                   pl.BlockSpec((tm, cw), lambda i: (i, 0))]
                  + _split_specs(tm, kv_lora, npb) + _split_specs(tm, rope, npb),
        out_shape=[jax.ShapeDtypeStruct((n // LANES, heads, LANES, cw), BF16),
                   jax.ShapeDtypeStruct((n, cw), BF16),
                   jax.ShapeDtypeStruct((ptok, kv_lora), F32),
                   jax.ShapeDtypeStruct((stok, kv_lora), F32),
                   jax.ShapeDtypeStruct((ptok, rope), F32),
                   jax.ShapeDtypeStruct((stok, rope), F32)],
        compiler_params=pltpu.CompilerParams(
            dimension_semantics=("arbitrary",), vmem_limit_bytes=VMEM_LIMIT),
        name="mla_prep",
    )(h[0], h[1], w_mlat, q_norm, kv_norm, w_uq3, w_ukt, cs)


def _attn_prompt_kernel(q_ref, k_ref, wuv_ref, o_ref, m_scr, l_scr, acc_scr,
                        *, tq, tk, heads, vdim, c, nchunk):
    i = pl.program_id(1)
    hpc = heads // nchunk
    rc = hpc * tq
    m_scr[...] = jnp.full(m_scr.shape, -jnp.inf, F32)
    l_scr[...] = jnp.zeros(l_scr.shape, F32)
    acc_scr[...] = jnp.zeros(acc_scr.shape, F32)

    assert tq == LANES and tk % tq == 0

    def step(off, width, masked):
        nt = width // LANES
        k = k_ref[pl.ds(off, width), :]
        v = k[:, :c]
        if masked:
            t_row = lax.broadcasted_iota(jnp.int32, (rc, 1), 0) & (tq - 1)
            keep = lax.broadcasted_iota(jnp.int32, (1, tq), 1) <= t_row

        def scores(r):
            q = q_ref[0, r * hpc:(r + 1) * hpc].reshape(rc, q_ref.shape[-1])
            s = _dot_nt(q, k)
            if masked:
                last = jnp.where(keep, s[:, width - tq:], -jnp.inf)
                s = last if nt == 1 else jnp.concatenate([s[:, :width - tq], last], axis=1)
            return s

        def update(r, s):
            rs = slice(r * rc, (r + 1) * rc)
            m_prev = m_scr[rs]
            smax = s[:, :LANES]
            for t in range(1, nt):
                smax = jnp.maximum(smax, s[:, t * LANES:(t + 1) * LANES])
            m_new = jnp.maximum(m_prev, jnp.max(smax, axis=-1, keepdims=True))
            alpha = jnp.exp2(m_prev - m_new)
            p = jnp.exp2(s - jnp.concatenate([m_new] * nt, axis=1))
            psum = p[:, :LANES]
            for t in range(1, nt):
                psum = psum + p[:, t * LANES:(t + 1) * LANES]
            l_scr[rs] = alpha * l_scr[rs] + psum
            acc_scr[rs] = jnp.concatenate([alpha] * (c // LANES), axis=1) * acc_scr[rs] + _dot(p.astype(BF16), v)
            m_scr[rs] = m_new

        s_cur = scores(0)
        for r in range(nchunk):
            s_next = scores(r + 1) if r + 1 < nchunk else None
            update(r, s_cur)
            s_cur = s_next

    ratio = tk // tq
    nfull = i // ratio

    def body(j, carry):
        step(pl.multiple_of(j * tk, tk), tk, False)
        return carry

    lax.fori_loop(0, nfull, body, 0)
    for rem in range(ratio):
        @pl.when(i % ratio == rem)
        def _():
            step(pl.multiple_of(nfull * tk, tk), (rem + 1) * tq, True)

    inv = 1.0 / jnp.sum(l_scr[...], axis=-1, keepdims=True)
    for h in range(heads):
        oh = (acc_scr[h * tq:(h + 1) * tq, :] * inv[h * tq:(h + 1) * tq]).astype(BF16)
        o_ref[:, h * vdim:(h + 1) * vdim] = _dot(oh, wuv_ref[:, h * vdim:(h + 1) * vdim]).astype(BF16)


def _attn_prompt(qc, kc, wuv, bsz, seq, *, heads, vdim, c):
    cw = kc.shape[1]
    tq = ATTN_TQ
    tk = min(ATTN_TK, seq)
    nq = seq // tq
    rows = heads * tq
    nchunk = ATTN_ROW_CHUNKS
    kern = functools.partial(_attn_prompt_kernel, tq=tq, tk=tk, heads=heads, vdim=vdim, c=c, nchunk=nchunk)
    return pl.pallas_call(
        kern,
        grid=(bsz, nq),
        in_specs=[pl.BlockSpec((1, heads, tq, cw), lambda bi, i: (bi * nq + i, 0, 0, 0)),
                  pl.BlockSpec((seq, cw), lambda bi, i: (bi, 0)),
                  _resident(wuv.shape)],
        out_specs=pl.BlockSpec((tq, heads * vdim), lambda bi, i: (bi * nq + i, 0)),
        out_shape=jax.ShapeDtypeStruct((bsz * seq, heads * vdim), BF16),
        scratch_shapes=[pltpu.VMEM((rows, LANES), F32), pltpu.VMEM((rows, LANES), F32),
                        pltpu.VMEM((rows, c), F32)],
        compiler_params=pltpu.CompilerParams(
            dimension_semantics=("parallel", "arbitrary"), vmem_limit_bytes=VMEM_LIMIT),
        name="attn_prompt",
    )(qc, kc, wuv)


def _attn_sample_kernel(pt_ref, q_ref, kn_ref, wuv_ref, cck_ref, cpe_ref, o_ref,
                        bufk, bufp, sem, m_scr, l_scr, acc_scr,
                        *, nseq, nsq, npages, ch, nbuf, heads, vdim, ntok, c, rope):
    cps = npages // ch
    total = (nseq // nsq) * cps

    def unit_copies(unit, slot):
        pair = unit // cps
        part = unit % cps
        copies = []
        for sq in range(nsq):
            base = (nsq * pair + sq) * npages + part * ch
            for pg in range(ch):
                page = pt_ref[base + pg]
                copies.append(pltpu.make_async_copy(
                    cck_ref.at[page], bufk.at[slot, sq, pl.ds(pg * PAGE_SIZE, PAGE_SIZE), :],
                    sem.at[0, slot]))
                copies.append(pltpu.make_async_copy(
                    cpe_ref.at[page], bufp.at[slot, sq, :, pl.ds(pg * PAGE_SIZE, PAGE_SIZE)],
                    sem.at[1, slot]))
        return copies

    def start_unit(unit):
        for cp in unit_copies(unit, unit % nbuf):
            cp.start()

    def wait_unit(unit):
        for cp in unit_copies(unit, unit % nbuf):
            cp.wait()

    for u0 in range(min(nbuf - 1, total)):
        start_unit(u0)

    def softmax_update(sq, s, v):
        m_prev = m_scr[sq]
        m_new = jnp.maximum(m_prev, jnp.max(s, axis=-1, keepdims=True))
        alpha = jnp.exp2(m_prev - m_new)
        p = jnp.exp2(s - m_new)
        l_scr[sq] = alpha * l_scr[sq] + jnp.sum(p, axis=-1, keepdims=True)
        acc_scr[sq] = alpha * acc_scr[sq] + _dot(p.astype(BF16), v)
        m_scr[sq] = m_new

    def body(u, carry):
        pair = u // cps
        part = u % cps
        slot = u % nbuf

        @pl.when(u + nbuf - 1 < total)
        def _():
            start_unit(u + nbuf - 1)

        wait_unit(u)

        @pl.when(part == 0)
        def _():
            m_scr[...] = jnp.full(m_scr.shape, -jnp.inf, F32)
            l_scr[...] = jnp.zeros(l_scr.shape, F32)
            acc_scr[...] = jnp.zeros(acc_scr.shape, F32)

        kbs, ss = [], []
        for sq in range(nsq):
            q = q_ref[nsq * pair + sq]
            kb = bufk[slot, sq].astype(BF16)
            kpt = bufp[slot, sq].astype(BF16)
            kbs.append(kb)
            ss.append(_dot_nt(q[:, :c], kb) + _dot(q[:, c:c + rope], kpt))
        for sq in range(nsq):
            softmax_update(sq, ss[sq], kbs[sq])

        @pl.when(part == cps - 1)
        def _():
            outs = []
            for sq in range(nsq):
                seq = nsq * pair + sq
                q = q_ref[seq]
                kn = kn_ref[seq]
                sn = _dot_nt(q, kn)
                rows, cols = sn.shape
                t_row = _div_pow2(lax.broadcasted_iota(jnp.int32, (rows, 1), 0), heads)
                col = lax.broadcasted_iota(jnp.int32, (1, cols), 1)
                sn = jnp.where((col <= t_row) & (col < ntok), sn, -jnp.inf)
                softmax_update(sq, sn, kn[:, :c])
                outs.append((acc_scr[sq] / l_scr[sq]).astype(BF16))
            y = _dot(jnp.concatenate(outs, axis=0), wuv_ref[...])
            h_row = lax.broadcasted_iota(jnp.int32, (nsq * rows, 1), 0) & (heads - 1)
            h_col = _div_pow2(lax.broadcasted_iota(jnp.int32, (1, heads * vdim), 1), vdim)
            z = jnp.where(h_row == h_col, y, 0.0)
            tid = lax.broadcasted_iota(jnp.int32, (ntok, 1), 0)
            for sq in range(nsq):
                out = jnp.zeros((ntok, heads * vdim), F32)
                for t in range(ntok):
                    r0 = sq * rows + t * heads
                    out = jnp.where(tid == t, jnp.sum(z[r0:r0 + heads], axis=0, keepdims=True), out)
                o_ref[nsq * pair + sq] = out

        return carry

    lax.fori_loop(0, total, body, 0)


def _attn_sample(page_table, qs, kn, wuv, cache_ckv, cache_kpet, *, heads, vdim, ntok, c):
    nseq, rows, _ = qs.shape
    npages = page_table.shape[1]
    rope = cache_kpet.shape[1]
    ch = min(SAMPLE_PAGES, npages)
    nsq = SAMPLE_SEQS
    assert npages % ch == 0 and nseq % nsq == 0
    nbuf = SAMPLE_BUFS
    kern = functools.partial(_attn_sample_kernel, nseq=nseq, nsq=nsq, npages=npages, ch=ch, nbuf=nbuf,
                             heads=heads, vdim=vdim, ntok=ntok, c=c, rope=rope)
    vmem = pl.BlockSpec(memory_space=pltpu.VMEM)
    return pl.pallas_call(
        kern,
        in_specs=[pl.BlockSpec(memory_space=pltpu.SMEM), vmem, vmem, vmem,
                  pl.BlockSpec(memory_space=pl.ANY), pl.BlockSpec(memory_space=pl.ANY)],
        out_specs=vmem,
        out_shape=jax.ShapeDtypeStruct((nseq, ntok, heads * vdim), F32),
        scratch_shapes=[pltpu.VMEM((nbuf, nsq, ch * PAGE_SIZE, c), F32),
                        pltpu.VMEM((nbuf, nsq, rope, ch * PAGE_SIZE), F32),
                        pltpu.SemaphoreType.DMA((2, nbuf)),
                        pltpu.VMEM((nsq, rows, 1), F32), pltpu.VMEM((nsq, rows, 1), F32),
                        pltpu.VMEM((nsq, rows, c), F32)],
        compiler_params=pltpu.CompilerParams(vmem_limit_bytes=VMEM_LIMIT),
        name="attn_sample",
    )(page_table.reshape(-1), qs, kn, wuv, cache_ckv, cache_kpet)


def _hgrn_kernel(hv_ref, lf_ref, s0_ref, gn_ref, ob_ref, so_ref,
                 s_scr, q_scr, k_scr, b_scr, at_scr,
                 *, L, G, sub, n_t, heads, dk, dv):
    c = pl.program_id(1)
    nc = pl.num_programs(1)

    @pl.when(c == 0)
    def _():
        s_scr[...] = s0_ref[...]

    for gi in range(G):
        _hgrn_chunk(hv_ref, lf_ref, gn_ref, ob_ref, s_scr, q_scr, k_scr, b_scr, at_scr,
                    gi, L=L, sub=sub, n_t=n_t, heads=heads, dk=dk, dv=dv)

    @pl.when(c == nc - 1)
    def _():
        so_ref[...] = s_scr[...]


def _hgrn_chunk(hv_ref, lf_ref, gn_ref, ob_ref, s_scr, q_scr, k_scr, b_scr, at_scr,
                gi, *, L, sub, n_t, heads, dk, dv):
    hd = heads * dk
    nsub = L // sub
    levels = int(math.log2(nsub))
    assert 2 ** levels == nsub
    rows = slice(gi * L, (gi + 1) * L)

    g = lf_ref[rows, :]
    g1 = g.astype(BF16)
    r1 = g - g1.astype(F32)
    g2 = r1.astype(BF16)
    g3 = (r1 - g2.astype(F32)).astype(BF16)
    row = lax.broadcasted_iota(jnp.int32, (L, L), 0)
    col = lax.broadcasted_iota(jnp.int32, (L, L), 1)
    tril = jnp.where(col <= row, 1.0, 0.0).astype(BF16)
    b = _dot(tril, g1) + _dot(tril, g2) + _dot(tril, g3)

    def group_bounds(grp):
        ends = [b[(j + 1) * grp - 1:(j + 1) * grp, :] for j in range(L // grp)]
        starts = [jnp.zeros_like(ends[0])] + ends[:-1]
        rep = lambda rows_: jnp.concatenate([jnp.broadcast_to(r, (grp, hd)) for r in rows_], axis=0)
        return rep(starts), rep(ends)

    bstart, bend = group_bounds(sub)
    brel = b - bstart
    blast = b[L - 1:L, :]

    q = hv_ref[rows, 0:hd].astype(F32)
    v = hv_ref[rows, hd:2 * hd]
    sg = hv_ref[rows, 2 * hd:3 * hd].astype(F32)
    k = hv_ref[rows, 3 * hd:4 * hd].astype(F32)

    q_scr[rows, :] = q
    k_scr[rows, :] = k
    b_scr[rows, :] = brel * math.log2(math.e)

    qt = q * jnp.exp(brel)
    kh = k * jnp.exp(bend - b)
    q_in = (qt * jnp.exp(bstart)).astype(BF16)
    k_st = (kh * jnp.exp(blast - bend)).astype(BF16)
    e_last = jnp.exp(blast)

    lhs = [qt.astype(BF16)]
    rhs = [kh.astype(BF16)]
    valid = []
    for lv in range(levels):
        grp = sub * (2 ** lv)
        if lv > 0:
            gs, ge = group_bounds(grp)
            lhs.append((qt * jnp.exp(bstart - gs)).astype(BF16))
            rhs.append((kh * jnp.exp(ge - bend)).astype(BF16))
        cg = _div_pow2(col, grp)
        valid.append(((cg & 1) == 1) & (_div_pow2(row, grp) == cg - 1))

    rid8 = lax.broadcasted_iota(jnp.int32, (8, 1), 0)
    lane = lax.broadcasted_iota(jnp.int32, (8, dk), 1)
    lss = [slice(h * dk, (h + 1) * dk) for h in range(heads)]
    halves = sub // 8

    def diag_body(i, carry):
        r0 = pl.multiple_of(gi * L + i * sub, sub)
        at = [[jnp.zeros((8, dk), F32) for _ in range(halves)] for _ in range(heads)]
        blk = [[ref[pl.ds(r0, sub), ls] for ref in (q_scr, k_scr, b_scr)] for ls in lss]
        for t in range(n_t):
            is_col = lane == i * sub + t
            for h in range(heads):
                qb, kb, bb = blk[h]
                for hf in range(t // 8 + 1):
                    rr = slice(8 * hf, 8 * hf + 8)
                    w = jnp.exp2(bb[t:t + 1, :] - bb[rr]) * kb[rr] * qb[t:t + 1, :]
                    a = jnp.sum(w, axis=-1, keepdims=True)
                    if t < 8 * hf + 7:
                        a = jnp.where(rid8 + 8 * hf <= t, a, 0.0)
                    at[h][hf] = jnp.where(is_col, a, at[h][hf])
        for h, ls in enumerate(lss):
            for hf in range(halves):
                at_scr[pl.ds(pl.multiple_of(r0 + 8 * hf, 8), 8), ls] = at[h][hf]
        return carry

    lax.fori_loop(0, nsub, diag_body, 0)

    gn = gn_ref[...]
    s_prev = [s_scr[gi, h] for h in range(heads)]
    o_in = [_dot(q_in[:, ls], s_prev[h].astype(BF16)) for h, ls in enumerate(lss)]
    p_off = [[_dot_nt(rhs[lv][:, ls], lhs[lv][:, ls]) for lv in range(levels)] for ls in lss]
    s_upd = [_dot_tn(k_st[:, ls], v[:, ls]) for ls in lss]
    o_mix = []
    for h, ls in enumerate(lss):
        if levels > 0:
            a_t = at_scr[rows, h * dk:h * dk + L]
            for lv in range(levels):
                a_t = a_t + jnp.where(valid[lv], p_off[h][lv], 0.0)
            o_mix.append(_dot_tn(a_t.astype(BF16), v[:, ls]))
        else:
            o_mix.append(_dot_tn(at_scr[rows, ls].astype(BF16), v[:, ls])[:L])
    for h, ls in enumerate(lss):
        decay = jnp.transpose(jnp.broadcast_to(e_last[:, ls], (dk, dk)))
        s_scr[gi, h] = decay * s_prev[h] + s_upd[h]
        o = o_in[h] + o_mix[h]
        on = o * lax.rsqrt(jnp.mean(o * o, axis=-1, keepdims=True) + EPS) * gn
        ob_ref[rows, ls] = (on * sg[:, ls]).astype(BF16)


def _hgrn(hv, lf, s0, gn, nc, *, L, G, sub, n_t, heads, dk, dv):
    nseq = s0.shape[0]
    rows = nseq * nc * L
    hd = heads * dk
    assert nseq % G == 0 and (G == 1 or nc == 1)
    kern = functools.partial(_hgrn_kernel, L=L, G=G, sub=sub, n_t=n_t, heads=heads, dk=dk, dv=dv)
    return pl.pallas_call(
        kern,
        grid=(nseq // G, nc),
        in_specs=[pl.BlockSpec((G * L, 4 * hd), lambda s, c: (s * nc + c, 0)),
                  pl.BlockSpec((G * L, hd), lambda s, c: (s * nc + c, 0)),
                  pl.BlockSpec((G, heads, dk, dv), lambda s, c: (s, 0, 0, 0)),
                  pl.BlockSpec((1, dv), lambda s, c: (0, 0))],
        out_specs=[pl.BlockSpec((G * L, heads * dv), lambda s, c: (s * nc + c, 0)),
                   pl.BlockSpec((G, heads, dk, dv), lambda s, c: (s, 0, 0, 0))],
        out_shape=[jax.ShapeDtypeStruct((rows, heads * dv), BF16),
                   jax.ShapeDtypeStruct((nseq, heads, dk, dv), F32)],
        scratch_shapes=[pltpu.VMEM((G, heads, dk, dv), F32)] + [pltpu.VMEM((G * L, hd), F32)] * 4,
        compiler_params=pltpu.CompilerParams(
            dimension_semantics=("parallel", "arbitrary"), vmem_limit_bytes=VMEM_LIMIT),
        name="hgrn",
    )(hv, lf, s0, gn)


def _merge_kernel(oap_ref, oas_ref, obp_ref, obs_ref, xp_ref, xs_ref, sg_ref, wa_ref, wb_ref, wo_ref, o_ref,
                  *, d, npb):
    a = _dot(_pair_rows(npb, oap_ref, oas_ref), wa_ref[...])
    b = _dot(_pair_rows(npb, obp_ref, obs_ref), wb_ref[...])
    merged = sg_ref[:, :d].astype(F32) * a + sg_ref[:, d:].astype(F32) * b
    o_ref[...] = _pair_rows(npb, xp_ref, xs_ref) + _dot(merged.astype(BF16), wo_ref[...])


def _merge(oa, ob, x, sg, wa, wb, wo):
    d = x[0].shape[1]
    tm, npb, nblk = _pair_tile(x, (256, 128))
    return pl.pallas_call(
        functools.partial(_merge_kernel, d=d, npb=npb),
        grid=(nblk,),
        in_specs=_split_specs(tm, oa[0].shape[1], npb) + _split_specs(tm, ob[0].shape[1], npb)
                 + _split_specs(tm, d, npb)
                 + [pl.BlockSpec((tm, 2 * d), lambda i: (i, 0)),
                    _resident(wa.shape), _resident(wb.shape), _resident(wo.shape)],
        out_specs=pl.BlockSpec((tm, d), lambda i: (i, 0)),
        out_shape=jax.ShapeDtypeStruct((nblk * tm, d), F32),
        compiler_params=pltpu.CompilerParams(
            dimension_semantics=("parallel",), vmem_limit_bytes=VMEM_LIMIT),
        name="merge_out",
    )(oa[0], oa[1], ob[0], ob[1], x[0], x[1], sg, wa, wb, wo)


def _ple_kernel(xp_ref, xs_ref, pp_ref, ps_ref, g_ref, wg_ref, wp_ref, gf_ref, yp_ref, ys_ref, *, final, npb):
    i = pl.program_id(0)
    x = _pair_rows(npb, xp_ref, xs_ref)
    gate = _sigmoid(_dot(_rms(x, g_ref[...]).astype(BF16), wg_ref[...]))
    p = _pair_rows(npb, pp_ref, ps_ref)
    y = x + gate * _dot(p.astype(BF16), wp_ref[...])
    if final:
        y = _rms(y, gf_ref[...])

    @pl.when(i < npb)
    def _():
        yp_ref[...] = y

    @pl.when(i >= npb)
    def _():
        ys_ref[...] = y


def _ple(x, pp, ps, g, wg, wp, gf, final):
    d = x[0].shape[1]
    ptok, stok = pp.shape[0], ps.shape[0]
    tm, npb, nblk = _pair_tile(x)
    return pl.pallas_call(
        functools.partial(_ple_kernel, final=final, npb=npb),
        grid=(nblk,),
        in_specs=_split_specs(tm, d, npb) + _split_specs(tm, pp.shape[1], npb)
                 + [_resident(g.shape), _resident(wg.shape), _resident(wp.shape), _resident(gf.shape)],
        out_specs=_split_specs(tm, d, npb),
        out_shape=[jax.ShapeDtypeStruct((ptok, d), F32), jax.ShapeDtypeStruct((stok, d), F32)],
        compiler_params=pltpu.CompilerParams(
            dimension_semantics=("arbitrary",), vmem_limit_bytes=VMEM_LIMIT),
        name="ple",
    )(x[0], x[1], pp, ps, g, wg, wp, gf)


def _rot_last(w):
    half = w.shape[-1] // 2
    return jnp.concatenate([-w[..., half:], w[..., :half]], axis=-1)


def _pad_to_lanes(w, axis):
    pads = [(0, 0)] * w.ndim
    pads[axis] = (0, LANES - w.shape[axis])
    return jnp.pad(w, pads)


def kernel(x_prompt, x_sample, cache_ckv, cache_kpe, state_hgrn, page_table, p_prompt, p_sample, ffn1_norm, ffn1_w_gate, ffn1_w_up, ffn1_w_down, mix_norm, w_in, q_norm, w_uq, kv_norm, w_uk, w_uv, hgrn_lb_logits, hgrn_out_norm, w_branch_a, w_branch_b, w_out, ffn2_norm, ffn2_w_gate, ffn2_w_up, ffn2_w_down, ple_norm, w_ple_gate, w_ple_proj, final_norm):
    bsz, seq, d = x_prompt.shape
    nseq, ntok, _ = x_sample.shape
    depth = w_in.shape[0]
    q_lora = q_norm.shape[1]
    kv_lora, heads, nope = w_uk.shape[1:]
    vdim = w_uv.shape[-1]
    rope = cache_kpe.shape[-1]
    hb, dk, dv = state_hgrn.shape[2:]
    hd = hb * dk
    npages = page_table.shape[1]
    past_len = npages * PAGE_SIZE
    ptok = bsz * seq
    stok = nseq * ntok
    scale = math.log2(math.e) / math.sqrt(nope + rope)
    assert rope <= LANES and nope == LANES and dk == LANES and dv == LANES

    x = (x_prompt.reshape(ptok, d), x_sample.reshape(stok, d))

    half = rope // 2
    inv = ROPE_THETA ** (-jnp.arange(half, dtype=F32) / half)
    pos = jnp.concatenate([jnp.tile(jnp.arange(seq, dtype=F32), bsz),
                           jnp.tile(jnp.arange(ntok, dtype=F32) + past_len, nseq)])
    ang = pos[:, None] * inv[None, :]
    cos = jnp.cos(ang)
    sin = jnp.sin(ang)
    cs = jnp.concatenate([_pad_to_lanes(jnp.concatenate([cos, cos], axis=1), 1),
                          _pad_to_lanes(jnp.concatenate([sin, sin], axis=1), 1)], axis=1)

    sp = [0]
    for w in (q_lora, kv_lora, rope, hd, hd, hb * dv, hb * dv, d, d):
        sp.append(sp[-1] + w)

    ckv_p, kpe_p, st_p, ckv_s, kpe_s, st_s = [], [], [], [], [], []
    for i in range(depth):
        wit = jnp.swapaxes(w_in[i], 0, 1)
        seg = [wit[sp[j]:sp[j + 1]] for j in range(9)]
        kr_t = seg[2]
        kr_rot_t = jnp.swapaxes(_rot_last(jnp.swapaxes(kr_t, 0, 1)), 0, 1)
        w_mlat = jnp.concatenate([seg[0], seg[1], _pad_to_lanes(kr_t, 0), _pad_to_lanes(kr_rot_t, 0)],
                                 axis=0).astype(BF16)
        w_hgt = jnp.concatenate([seg[3], seg[5], seg[6], seg[4]], axis=0).astype(BF16)
        w_gtt = jnp.concatenate([seg[7], seg[8]], axis=0).astype(BF16)
        wq = w_uq[i].reshape(q_lora, heads, nope + rope)
        wq_rope = wq[:, :, nope:]
        w_uq3 = jnp.concatenate([wq[:, :, :nope].reshape(q_lora, heads * nope),
                                 _pad_to_lanes(wq_rope, 2).reshape(q_lora, heads * LANES),
                                 _pad_to_lanes(_rot_last(wq_rope), 2).reshape(q_lora, heads * LANES)],
                                axis=1).astype(BF16)
        w_ukt = jnp.transpose(w_uk[i], (1, 2, 0)).astype(BF16)
        w_uv2 = w_uv[i].reshape(kv_lora, heads * vdim).astype(BF16)

        (x_s, hmix_s), wb = _ffn(x[1], ffn1_norm[i][None], ffn1_w_gate[i], ffn1_w_up[i], ffn1_w_down[i],
                                 mix_norm[i][None])
        x_p, hmix_p = _ffn(x[0], ffn1_norm[i][None], *wb, mix_norm[i][None])
        x, hmix = (x_p, x_s), (hmix_p, hmix_s)

        sg = _gates(hmix, w_gtt)
        hv, lf = _hgrn_proj(hmix, w_hgt, hgrn_lb_logits, i)
        qc, kc, ckv_fp, ckv_fs, kpe_fp, kpe_fs = _mla_prep(
            hmix, w_mlat, q_norm[i][None], kv_norm[i][None], w_uq3, w_ukt, cs,
            heads=heads, q_lora=q_lora, kv_lora=kv_lora, nope=nope, rope=rope, scale=scale)

        npb = ptok // LANES
        oa_p = _attn_prompt(qc, kc, w_uv2, bsz, seq, heads=heads, vdim=vdim, c=kv_lora)
        cw = qc.shape[-1]
        qs = jnp.transpose(qc[npb:], (0, 2, 1, 3)).reshape(nseq, ntok * heads, cw)
        kn = jnp.pad(kc[ptok:].reshape(nseq, ntok, cw), ((0, 0), (0, 16 - ntok), (0, 0)))
        oa_s = _attn_sample(page_table, qs, kn, w_uv2, cache_ckv[i], jnp.swapaxes(cache_kpe[i], 1, 2),
                            heads=heads, vdim=vdim, ntok=ntok, c=kv_lora)
        oa = (oa_p, oa_s.reshape(stok, heads * vdim).astype(BF16))

        gn = hgrn_out_norm[i][None]
        ob_p, s_p = _hgrn(hv, lf, jnp.zeros((bsz, hb, dk, dv), F32), gn, seq // HGRN_CHUNK,
                          L=HGRN_CHUNK, G=1, sub=HGRN_SUB, n_t=HGRN_SUB, heads=hb, dk=dk, dv=dv)
        srows = HGRN_SAMPLE_ROWS
        pad = srows - ntok
        hv_s = jnp.pad(hv[ptok:].reshape(nseq, ntok, 4 * hd), ((0, 0), (0, pad), (0, 0)))
        lf_s = jnp.pad(lf[ptok:].reshape(nseq, ntok, hd), ((0, 0), (0, pad), (0, 0)))
        ob_s, s_s = _hgrn(hv_s.reshape(nseq * srows, 4 * hd), lf_s.reshape(nseq * srows, hd),
                          state_hgrn[i], gn, 1, L=srows, G=HGRN_SAMPLE_GROUP, sub=srows, n_t=ntok,
                          heads=hb, dk=dk, dv=dv)
        ob = (ob_p, ob_s.reshape(nseq, srows, hb * dv)[:, :ntok].reshape(stok, hb * dv))

        x = _merge(oa, ob, x, sg, w_branch_a[i].astype(BF16), w_branch_b[i].astype(BF16),
                   w_out[i].astype(BF16))
        x_s, wb = _ffn(x, ffn2_norm[i][None], ffn2_w_gate[i], ffn2_w_up[i], ffn2_w_down[i],
                       row0=ptok, nrows=stok)
        x = (_ffn(x, ffn2_norm[i][None], *wb, nrows=ptok), x_s)
        x = tuple(_ple(x, p_prompt[i].reshape(ptok, -1), p_sample[i].reshape(stok, -1), ple_norm[i][None],
                       w_ple_gate[i].astype(BF16), w_ple_proj[i].astype(BF16), final_norm[None],
                       i == depth - 1))

        ckv_p.append(ckv_fp.reshape(bsz, seq, kv_lora))
        kpe_p.append(kpe_fp.reshape(bsz, seq, rope))
        st_p.append(s_p)
        ckv_s.append(ckv_fs.reshape(nseq, ntok, kv_lora))
        kpe_s.append(kpe_fs.reshape(nseq, ntok, rope))
        st_s.append(s_s)

    return (x[0].reshape(bsz, seq, d), x[1].reshape(nseq, ntok, d),
            jnp.stack(ckv_p), jnp.stack(kpe_p), jnp.stack(st_p),
            jnp.stack(ckv_s), jnp.stack(kpe_s), jnp.stack(st_s))
```
